```python
import math
import jax
import jax.numpy as jnp
from jax import lax
import numpy as np

D_MODEL = 4096
BATCH = 1
SEQ = 8192
DEPTH = 2
DEC_BATCH = 8
DEC_SEQ = 64
PAST_LEN = 1024

CHUNK = 64
RMS_EPS = 1e-6
ROPE_THETA = 10000.0

N_A = DEPTH // 2
N_B = DEPTH - N_A
N_DENSE = (DEPTH + 1) // 2
N_MOE = DEPTH // 2

GLA_HEADS = 8
GLA_DK = D_MODEL // 2 // GLA_HEADS
GLA_DV = D_MODEL // GLA_HEADS
GLA_GATE_RANK = 16
GLA_GATE_NORM = 16.0
QK_A = GLA_HEADS * GLA_DK
V_A = GLA_HEADS * GLA_DV
GLA_IN = 2 * QK_A + 2 * V_A + GLA_GATE_RANK

SWA_HEAD_DIM = 64
SWA_HEADS = D_MODEL // SWA_HEAD_DIM
SWA_KV_HEADS = 8
SWA_GROUP = SWA_HEADS // SWA_KV_HEADS
WINDOW = 128
WIN_CHUNKS = WINDOW // CHUNK
Q_B = SWA_HEADS * SWA_HEAD_DIM
KV_B = SWA_KV_HEADS * SWA_HEAD_DIM

FF_DIM = D_MODEL * 7 // 2
N_EXPERTS = 8
TOP_K = 2
MOE_BLOCK = 128

kernel_name = 'yoco_gla_swa_sink_moe_stream_step'


def rmsnorm(x, g):
    x32 = x.astype(jnp.float32)
    y = x32 * lax.rsqrt(jnp.mean(x32 * x32, axis=-1, keepdims=True) + RMS_EPS)
    return (y * g.astype(jnp.float32)).astype(x.dtype)


def rope(x, pos):
    half = x.shape[-1] // 2
    inv_freq = ROPE_THETA ** (-jnp.arange(half, dtype=jnp.float32) / half)
    ang = pos.astype(jnp.float32)[:, None] * inv_freq[None, :]
    cos = jnp.cos(ang)[:, None, :]
    sin = jnp.sin(ang)[:, None, :]
    x32 = x.astype(jnp.float32)
    x1, x2 = x32[..., :half], x32[..., half:]
    return jnp.concatenate([x1 * cos - x2 * sin, x2 * cos + x1 * sin], axis=-1).astype(x.dtype)


def swiglu(x, w_gate, w_up, w_down):
    return (jax.nn.silu(x @ w_gate) * (x @ w_up)) @ w_down


def gla_recurrence(q, k, v, log_a, s0):
    b_sz, t_len = q.shape[:2]
    c = CHUNK if t_len % CHUNK == 0 else t_len
    n = t_len // c

    def to_chunks(a):
        return a.reshape(b_sz, n, c, a.shape[2], a.shape[3]).transpose(1, 0, 3, 2, 4)

    causal = jnp.tril(jnp.ones((c, c), dtype=bool))[:, :, None]

    def step(s, inp):
        qc, kc, vc, gc = inp
        cum = jnp.cumsum(gc, axis=2)
        diff = cum[:, :, :, None, :] - cum[:, :, None, :, :]
        decay = jnp.exp(jnp.where(causal, diff, -jnp.inf))
        attn = jnp.einsum('bhijd,bhjd->bhij', qc[:, :, :, None, :] * decay, kc)
        o = (jnp.einsum('bhid,bhde->bhie', qc * jnp.exp(cum), s)
             + jnp.einsum('bhij,bhje->bhie', attn, vc))
        last = cum[:, :, -1:, :]
        s_new = (jnp.exp(last[:, :, 0, :])[..., None] * s
                 + jnp.einsum('bhjd,bhje->bhde', kc * jnp.exp(last - cum), vc))
        return s_new, o

    s_fin, o = lax.scan(step, s0.astype(jnp.float32),
                        (to_chunks(q), to_chunks(k), to_chunks(v), to_chunks(log_a)))
    o = o.transpose(1, 0, 3, 2, 4).reshape(b_sz, t_len, q.shape[2], v.shape[3])
    return o, s_fin


def gla_mixer(hn, w_in, w_a2, b_a, g_norm, w_o, s0):
    b_sz, t_len, _ = hn.shape
    proj = hn @ w_in
    q, k, v, g, a = jnp.split(proj, [QK_A, 2 * QK_A, 2 * QK_A + V_A, 2 * QK_A + 2 * V_A], axis=-1)
    log_a = jax.nn.log_sigmoid((a @ w_a2 + b_a).astype(jnp.float32)) / GLA_GATE_NORM

    def heads(t, d):
        return t.reshape(b_sz, t_len, GLA_HEADS, d).astype(jnp.float32)

    o, s = gla_recurrence(heads(q, GLA_DK) * (GLA_DK ** -0.5), heads(k, GLA_DK),
                          heads(v, GLA_DV), log_a.reshape(b_sz, t_len, GLA_HEADS, GLA_DK), s0)
    o = rmsnorm(o, g_norm) * jax.nn.silu(heads(g, GLA_DV))
    return o.reshape(b_sz, t_len, V_A).astype(hn.dtype) @ w_o, s


def moe_ffn(x, router, w_gate, w_up, w_down):
    b_sz, t_len, d = x.shape
    xt = x.reshape(b_sz * t_len, d)
    n_tok = xt.shape[0]
    logits = xt.astype(jnp.float32) @ router.astype(jnp.float32)
    top_val, top_idx = lax.top_k(logits, TOP_K)
    gates = jax.nn.softmax(top_val, axis=-1)
    n_assign = n_tok * TOP_K
    flat_e = top_idx.reshape(-1).astype(jnp.int32)
    flat_t = jnp.arange(n_assign, dtype=jnp.int32) // TOP_K
    flat_g = gates.reshape(-1)
    order = jnp.argsort(flat_e)
    se, st, sg = flat_e[order], flat_t[order], flat_g[order]
    counts = jnp.bincount(flat_e, length=N_EXPERTS).astype(jnp.int32)
    starts = jnp.cumsum(counts) - counts
    padded = (counts + MOE_BLOCK - 1) // MOE_BLOCK * MOE_BLOCK
    pends = jnp.cumsum(padded)
    pstarts = pends - padded
    dest = pstarts[se] + jnp.arange(n_assign, dtype=jnp.int32) - starts[se]
    n_blocks = -(-n_assign // MOE_BLOCK) + N_EXPERTS
    n_slots = n_blocks * MOE_BLOCK
    slot_tok = jnp.full((n_slots,), n_tok, jnp.int32).at[dest].set(st)
    x_pad = jnp.concatenate([xt, jnp.zeros((1, d), xt.dtype)], axis=0)
    xb = x_pad[slot_tok].reshape(n_blocks, MOE_BLOCK, d)
    block_e = jnp.clip(jnp.searchsorted(pends, jnp.arange(n_blocks, dtype=jnp.int32) * MOE_BLOCK,
                                        side='right'), 0, N_EXPERTS - 1)

    def expert_block(args):
        xblk, e = args
        return swiglu(xblk, w_gate[e], w_up[e], w_down[e])

    yb = lax.map(expert_block, (xb, block_e)).reshape(n_slots, d)
    y = jax.ops.segment_sum(yb[dest] * sg[:, None].astype(yb.dtype), st, num_segments=n_tok)
    return y.reshape(b_sz, t_len, d)


def shared_kv(h, ln_kv, kv_w, kv_b, pos):
    b_sz, t_len, _ = h.shape
    kv = rmsnorm(h, ln_kv) @ kv_w + kv_b
    k, v = jnp.split(kv, [KV_B], axis=-1)
    k = rope(k.reshape(b_sz, t_len, SWA_KV_HEADS, SWA_HEAD_DIM), pos)
    v = v.reshape(b_sz, t_len, SWA_KV_HEADS, SWA_HEAD_DIM)
    return k, v


def band_blocks(x):
    b_sz, t_len = x.shape[:2]
    n = t_len // CHUNK
    xp = jnp.pad(x, ((0, 0), (WIN_CHUNKS * CHUNK, 0), (0, 0), (0, 0)))
    xp = xp.reshape(b_sz, n + WIN_CHUNKS, CHUNK, x.shape[2], x.shape[3])
    return jnp.concatenate([xp[:, w:w + n] for w in range(WIN_CHUNKS + 1)], axis=2)


def sink_attention(q, k, v, valid, sinks):
    s = jnp.einsum('bnqkgd,bnskd->bnkgqs', q.astype(jnp.float32), k.astype(jnp.float32)) * (SWA_HEAD_DIM ** -0.5)
    s = jnp.where(valid[None, :, None, None, None, :], s, -jnp.inf)
    sink = sinks.astype(jnp.float32)[None, None, :, :, None, None]
    m = jnp.maximum(jnp.max(s, axis=-1, keepdims=True), sink)
    p = jnp.exp(s - m)
    denom = jnp.sum(p, axis=-1, keepdims=True) + jnp.exp(sink - m)
    o = jnp.einsum('bnkgqs,bnskd->bnqkgd', p / denom, v.astype(jnp.float32))
    return o.astype(q.dtype)


def swa_mixer(hn, pos, kb, vb, valid, w_q, b_q, sinks, w_o, b_o):
    b_sz, t_len, _ = hn.shape
    q = rope((hn @ w_q + b_q).reshape(b_sz, t_len, SWA_HEADS, SWA_HEAD_DIM), pos)
    nq = kb.shape[1]
    q = q.reshape(b_sz, nq, t_len // nq, SWA_KV_HEADS, SWA_GROUP, SWA_HEAD_DIM)
    o = sink_attention(q, kb, vb, valid, sinks.reshape(SWA_KV_HEADS, SWA_GROUP))
    return o.reshape(b_sz, t_len, Q_B) @ w_o + b_o


def trunk(x, pos, gla_s0, kv_cache, w):
    h = x
    gla_states = []
    kb = vb = valid = new_k = new_v = None
    for l in range(DEPTH):
        hn = rmsnorm(h, w['ln_mix'][l])
        if l < N_A:
            mix, s = gla_mixer(hn, w['gla_w_in'][l], w['gla_w_a2'][l], w['gla_b_a'][l],
                               w['gla_norm'][l], w['gla_w_o'][l], gla_s0[l])
            gla_states.append(s)
        else:
            if l == N_A:
                k, v = shared_kv(h, w['ln_kv'], w['kv_w'], w['kv_b'], pos)
                if kv_cache is None:
                    kb, vb = band_blocks(k), band_blocks(v)
                    n = k.shape[1] // CHUNK
                    key_chunk = (jnp.arange(n)[:, None] - WIN_CHUNKS
                                 + jnp.arange((WIN_CHUNKS + 1) * CHUNK)[None, :] // CHUNK)
                    valid = key_chunk >= 0
                    keep = min(WINDOW, k.shape[1])
                    new_k, new_v = k[:, -keep:], v[:, -keep:]
                else:
                    ck, cv = kv_cache
                    k_all = jnp.concatenate([ck.astype(k.dtype), k], axis=1)
                    v_all = jnp.concatenate([cv.astype(v.dtype), v], axis=1)
                    kb, vb = k_all[:, None], v_all[:, None]
                    valid = jnp.ones((1, k_all.shape[1]), dtype=bool)
                    keep = ck.shape[1]
                    new_k, new_v = k_all[:, -keep:], v_all[:, -keep:]
            j = l - N_A
            mix = swa_mixer(hn, pos, kb, vb, valid, w['swa_w_q'][j], w['swa_b_q'][j],
                            w['swa_sinks'][j], w['swa_w_o'][j], w['swa_b_o'][j])
        h = h + mix
        hn = rmsnorm(h, w['ln_ff'][l])
        if l % 2 == 0:
            i = l // 2
            ff = swiglu(hn, w['ffd_w_gate'][i], w['ffd_w_up'][i], w['ffd_w_down'][i])
        else:
            i = l // 2
            ff = moe_ffn(hn, w['moe_router'][i], w['moe_w_gate'][i], w['moe_w_up'][i], w['moe_w_down'][i])
        h = h + ff
    return rmsnorm(h, w['ln_out']), jnp.stack(gla_states), new_k, new_v


def setup_inputs(seed: int = 0) -> dict:
    key = jax.random.key(seed)
    ks = jax.random.split(key, 32)
    f32 = jnp.float32

    def nrm(k, shape, fan_in):
        return jax.random.normal(k, shape, f32) * (fan_in ** -0.5)

    def gain(k, shape):
        return 1.0 + 0.05 * jax.random.normal(k, shape, f32)

    def small(k, shape, scale):
        return scale * jax.random.normal(k, shape, f32)

    cache_rows = min(WINDOW, PAST_LEN)
    return {
        'x_prompt': jax.random.normal(ks[0], (BATCH, SEQ, D_MODEL), f32),
        'x_sample': jax.random.normal(ks[1], (DEC_BATCH, DEC_SEQ, D_MODEL), f32),
        'state_gla': jax.random.normal(ks[2], (N_A, DEC_BATCH, GLA_HEADS, GLA_DK, GLA_DV), f32),
        'cache_swa_k': jax.random.normal(ks[3], (DEC_BATCH, cache_rows, SWA_KV_HEADS, SWA_HEAD_DIM), f32),
        'cache_swa_v': jax.random.normal(ks[4], (DEC_BATCH, cache_rows, SWA_KV_HEADS, SWA_HEAD_DIM), f32),
        'ln_mix': gain(ks[5], (DEPTH, D_MODEL)),
        'ln_ff': gain(ks[6], (DEPTH, D_MODEL)),
        'ln_kv': gain(ks[7], (D_MODEL,)),
        'ln_out': gain(ks[8], (D_MODEL,)),
        'gla_w_in': nrm(ks[9], (N_A, D_MODEL, GLA_IN), D_MODEL),
        'gla_w_a2': nrm(ks[10], (N_A, GLA_GATE_RANK, QK_A), GLA_GATE_RANK),
        'gla_b_a': small(ks[11], (N_A, QK_A), 0.1),
        'gla_norm': gain(ks[12], (N_A, GLA_DV)),
        'gla_w_o': nrm(ks[13], (N_A, V_A, D_MODEL), V_A),
        'kv_w': nrm(ks[14], (D_MODEL, 2 * KV_B), D_MODEL),
        'kv_b': small(ks[15], (2 * KV_B,), 0.02),
        'swa_w_q': nrm(ks[16], (N_B, D_MODEL, Q_B), D_MODEL),
        'swa_b_q': small(ks[17], (N_B, Q_B), 0.02),
        'swa_sinks': small(ks[18], (N_B, SWA_HEADS), 0.5),
        'swa_w_o': nrm(ks[19], (N_B, Q_B, D_MODEL), Q_B),
        'swa_b_o': small(ks[20], (N_B, D_MODEL), 0.02),
        'ffd_w_gate': nrm(ks[21], (N_DENSE, D_MODEL, FF_DIM), D_MODEL),
        'ffd_w_up': nrm(ks[22], (N_DENSE, D_MODEL, FF_DIM), D_MODEL),
        'ffd_w_down': nrm(ks[23], (N_DENSE, FF_DIM, D_MODEL), FF_DIM),
        'moe_router': nrm(ks[24], (N_MOE, D_MODEL, N_EXPERTS), D_MODEL),
        'moe_w_gate': nrm(ks[25], (N_MOE, N_EXPERTS, D_MODEL, FF_DIM), D_MODEL),
        'moe_w_up': nrm(ks[26], (N_MOE, N_EXPERTS, D_MODEL, FF_DIM), D_MODEL),
        'moe_w_down': nrm(ks[27], (N_MOE, N_EXPERTS, FF_DIM, D_MODEL), FF_DIM),
    }


def reference(x_prompt, x_sample, state_gla, cache_swa_k, cache_swa_v,
              ln_mix, ln_ff, ln_kv, ln_out,
              gla_w_in, gla_w_a2, gla_b_a, gla_norm, gla_w_o,
              kv_w, kv_b, swa_w_q, swa_b_q, swa_sinks, swa_w_o, swa_b_o,
              ffd_w_gate, ffd_w_up, ffd_w_down,
              moe_router, moe_w_gate, moe_w_up, moe_w_down):
    w = dict(ln_mix=ln_mix, ln_ff=ln_ff, ln_kv=ln_kv, ln_out=ln_out,
             gla_w_in=gla_w_in, gla_w_a2=gla_w_a2, gla_b_a=gla_b_a, gla_norm=gla_norm, gla_w_o=gla_w_o,
             kv_w=kv_w, kv_b=kv_b, swa_w_q=swa_w_q, swa_b_q=swa_b_q, swa_sinks=swa_sinks,
             swa_w_o=swa_w_o, swa_b_o=swa_b_o,
             ffd_w_gate=ffd_w_gate, ffd_w_up=ffd_w_up, ffd_w_down=ffd_w_down,
             moe_router=moe_router, moe_w_gate=moe_w_gate, moe_w_up=moe_w_up, moe_w_down=moe_w_down)

    b_p, t_p, _ = x_prompt.shape
    pos_p = jnp.arange(t_p, dtype=jnp.int32)
    s0_p = jnp.zeros((N_A, b_p, GLA_HEADS, GLA_DK, GLA_DV), jnp.float32)
    y_prompt, state_gla_prompt, cache_swa_k_prompt, cache_swa_v_prompt = trunk(
        x_prompt, pos_p, s0_p, None, w)

    t_s = x_sample.shape[1]
    pos_s = PAST_LEN + jnp.arange(t_s, dtype=jnp.int32)
    y_sample, state_gla_sample, cache_swa_k_sample, cache_swa_v_sample = trunk(
        x_sample, pos_s, state_gla, (cache_swa_k, cache_swa_v), w)

    return (y_prompt, y_sample, state_gla_prompt, cache_swa_k_prompt, cache_swa_v_prompt,
            state_gla_sample, cache_swa_k_sample, cache_swa_v_sample)
```

```python
import functools

import jax
import jax.numpy as jnp
from jax import lax
from jax.experimental import pallas as pl
from jax.experimental.pallas import tpu as pltpu

F32 = jnp.float32
BF16 = jnp.bfloat16

D_MODEL = 4096
SEQ = 8192
DEC_BATCH = 8
DEC_SEQ = 64
PAST_LEN = 1024
CHUNK = 64
RMS_EPS = 1e-6
ROPE_THETA = 10000.0

GLA_HEADS = 8
GLA_DK = 256
GLA_DV = 512
GLA_GATE_RANK = 16
GLA_GATE_NORM = 16.0
QK_A = GLA_HEADS * GLA_DK
V_A = GLA_HEADS * GLA_DV
GLA_MAIN = 2 * QK_A + 2 * V_A

SWA_HEAD_DIM = 64
SWA_HEADS = 64
SWA_KV_HEADS = 8
SWA_GROUP = SWA_HEADS // SWA_KV_HEADS
WINDOW = 128
WIN_CHUNKS = WINDOW // CHUNK
KV_B = SWA_KV_HEADS * SWA_HEAD_DIM

FF_DIM = 14336
N_EXPERTS = 8
TOP_K = 2

T_TOK = SEQ + DEC_BATCH * DEC_SEQ
N_PROMPT_CHUNKS = SEQ // CHUNK
N_CHUNKS = T_TOK // CHUNK
LANE = 128

VMEM_LIMIT = 56 * 1024 * 1024

TM_TOK = 1088
TM_MOE = 544


def _cparams(n_axes):
    return pltpu.CompilerParams(dimension_semantics=("arbitrary",) * n_axes,
                                vmem_limit_bytes=VMEM_LIMIT)


def _rmsnorm_body(*refs, n_out, has_delta, emit_sum, has_router):
    it = iter(refs)
    x_ref = next(it)
    d_ref = next(it) if has_delta else None
    g_ref = next(it)
    r_ref = next(it) if has_router else None
    sum_ref = next(it) if emit_sum else None
    o_refs = [next(it) for _ in range(n_out)]
    lg_ref = next(it) if has_router else None

    x = x_ref[...]
    if has_delta:
        x = x + d_ref[...].astype(F32)
    if emit_sum:
        sum_ref[...] = x
    y = x * lax.rsqrt(jnp.mean(x * x, axis=-1, keepdims=True) + RMS_EPS)
    for i, o_ref in enumerate(o_refs):
        yi = y * g_ref[i:i + 1, :]
        o_ref[...] = yi.astype(o_ref.dtype)
        if has_router and i == 0:
            lg_ref[...] = jnp.dot(yi, r_ref[...], precision=lax.Precision.HIGHEST,
                                  preferred_element_type=F32)


def _rmsnorm(x, gains, *, delta=None, emit_sum=False, router=None, out_dtype=BF16,
             row_off=0, n_rows=None, tr=256):
    n_rows = x.shape[0] if n_rows is None else n_rows
    d = x.shape[1]
    n_out = gains.shape[0]
    off = row_off // tr
    row_spec = pl.BlockSpec((tr, d), lambda i: (i + off, 0))
    out_spec = pl.BlockSpec((tr, d), lambda i: (i, 0))
    in_specs = [row_spec]
    args = [x]
    if delta is not None:
        in_specs.append(row_spec)
        args.append(delta)
    in_specs.append(pl.BlockSpec((n_out, d), lambda i: (0, 0)))
    args.append(gains)
    if router is not None:
        in_specs.append(pl.BlockSpec(router.shape, lambda i: (0, 0)))
        args.append(router)
    out_shape, out_specs = [], []
    if emit_sum:
        out_shape.append(jax.ShapeDtypeStruct((n_rows, d), F32))
        out_specs.append(out_spec)
    for _ in range(n_out):
        out_shape.append(jax.ShapeDtypeStruct((n_rows, d), out_dtype))
        out_specs.append(out_spec)
    if router is not None:
        out_shape.append(jax.ShapeDtypeStruct((n_rows, router.shape[1]), F32))
        out_specs.append(pl.BlockSpec((tr, router.shape[1]), lambda i: (i, 0)))
    body = functools.partial(_rmsnorm_body, n_out=n_out, has_delta=delta is not None,
                             emit_sum=emit_sum, has_router=router is not None)
    return pl.pallas_call(
        body, grid=(n_rows // tr,), in_specs=in_specs, out_specs=out_specs,
        out_shape=out_shape, compiler_params=_cparams(1), name="rmsnorm")(*args)


def _rope_tile(acc, cos_ref, sin_ref):
    cos = cos_ref[...]
    sin = sin_ref[...]
    lane = lax.broadcasted_iota(jnp.int32, (acc.shape[0], LANE), 1)
    first_half = (lane % SWA_HEAD_DIM) < (SWA_HEAD_DIM // 2)
    outs = []
    for j in range(acc.shape[1] // LANE):
        blk = acc[:, j * LANE:(j + 1) * LANE]
        rot = jnp.where(first_half,
                        pltpu.roll(blk, LANE - SWA_HEAD_DIM // 2, 1),
                        pltpu.roll(blk, SWA_HEAD_DIM // 2, 1))
        outs.append(blk * cos + rot * sin)
    return jnp.concatenate(outs, axis=1)


def _mm_body(*refs, has_bias, has_res, has_rope):
    it = iter(refs)
    x_ref = next(it)
    w_ref = next(it)
    b_ref = next(it) if has_bias else None
    cos_ref = next(it) if has_rope else None
    sin_ref = next(it) if has_rope else None
    res_ref = next(it) if has_res else None
    o_ref = next(it)
    wbf_ref = next(it)

    @pl.when(pl.program_id(1) == 0)
    def _():
        wbf_ref[...] = w_ref[...].astype(BF16)

    acc = jnp.dot(x_ref[...], wbf_ref[...], preferred_element_type=F32)
    if has_bias:
        acc = acc + b_ref[...]
    if has_rope:
        acc = _rope_tile(acc, cos_ref, sin_ref)
    if has_res:
        acc = acc + res_ref[...]
    o_ref[...] = acc.astype(o_ref.dtype)


def _mm(x, w, *, n_cols, tn, tm=TM_TOK, col_off=0, bias=None, res=None, rope=None,
        out_dtype=F32):
    m, k = x.shape
    cb = col_off // tn
    in_specs = [pl.BlockSpec((tm, k), lambda n, i: (i, 0)),
                pl.BlockSpec((k, tn), lambda n, i: (0, n + cb))]
    args = [x, w]
    if bias is not None:
        in_specs.append(pl.BlockSpec((1, tn), lambda n, i: (0, n + cb)))
        args.append(bias)
    if rope is not None:
        for t in rope:
            in_specs.append(pl.BlockSpec((tm, LANE), lambda n, i: (i, 0)))
            args.append(t)
    if res is not None:
        in_specs.append(pl.BlockSpec((tm, tn), lambda n, i: (i, n)))
        args.append(res)
    body = functools.partial(_mm_body, has_bias=bias is not None, has_res=res is not None,
                             has_rope=rope is not None)
    return pl.pallas_call(
        body, grid=(n_cols // tn, m // tm), in_specs=in_specs,
        out_specs=pl.BlockSpec((tm, tn), lambda n, i: (i, n)),
        out_shape=jax.ShapeDtypeStruct((m, n_cols), out_dtype),
        scratch_shapes=[pltpu.VMEM((k, tn), BF16)],
        compiler_params=_cparams(2), name="matmul_ws")(*args)


def _swiglu_tile(x, wg, wu):
    g = jnp.dot(x, wg, preferred_element_type=F32)
    u = jnp.dot(x, wu, preferred_element_type=F32)
    return g * jax.nn.sigmoid(g) * u


def _gateup_body(x_ref, wg_ref, wu_ref, o_ref, wgbf_ref, wubf_ref):
    @pl.when(pl.program_id(1) == 0)
    def _():
        wgbf_ref[...] = wg_ref[...].astype(BF16)
        wubf_ref[...] = wu_ref[...].astype(BF16)

    o_ref[...] = _swiglu_tile(x_ref[...], wgbf_ref[...], wubf_ref[...]).astype(o_ref.dtype)


def _gateup(x, wg, wu, *, tm=TM_TOK, tn=256):
    m, k = x.shape
    f = wg.shape[1]
    return pl.pallas_call(
        _gateup_body, grid=(f // tn, m // tm),
        in_specs=[pl.BlockSpec((tm, k), lambda n, i: (i, 0)),
                  pl.BlockSpec((k, tn), lambda n, i: (0, n)),
                  pl.BlockSpec((k, tn), lambda n, i: (0, n))],
        out_specs=pl.BlockSpec((tm, tn), lambda n, i: (i, n)),
        out_shape=jax.ShapeDtypeStruct((m, f), BF16),
        scratch_shapes=[pltpu.VMEM((k, tn), BF16), pltpu.VMEM((k, tn), BF16)],
        compiler_params=_cparams(2), name="ffn_gateup")(x, wg, wu)


def _moe_gateup_body(te_ref, src_ref, first_ref, valid_ref,
                     x_ref, wg_ref, wu_ref, o_ref, wgbf_ref, wubf_ref):
    r = pl.program_id(1)

    @pl.when(first_ref[r] == 1)
    def _():
        wgbf_ref[...] = wg_ref[...].astype(BF16)
        wubf_ref[...] = wu_ref[...].astype(BF16)

    @pl.when(valid_ref[r] == 1)
    def _():
        o_ref[...] = _swiglu_tile(x_ref[...], wgbf_ref[...], wubf_ref[...]).astype(o_ref.dtype)


def _moe_gateup(meta, xs, wg, wu, *, tm=TM_MOE, tn=256):
    n_slots, k = xs.shape
    f = wg.shape[2]
    grid_spec = pltpu.PrefetchScalarGridSpec(
        num_scalar_prefetch=4, grid=(f // tn, n_slots // tm),
        in_specs=[pl.BlockSpec((tm, k), lambda n, r, te, src, fi, va: (src[r], 0)),
                  pl.BlockSpec((None, k, tn), lambda n, r, te, src, fi, va: (te[r], 0, n)),
                  pl.BlockSpec((None, k, tn), lambda n, r, te, src, fi, va: (te[r], 0, n))],
        out_specs=pl.BlockSpec((tm, tn), lambda n, r, te, src, fi, va: (src[r], n)),
        scratch_shapes=[pltpu.VMEM((k, tn), BF16), pltpu.VMEM((k, tn), BF16)])
    return pl.pallas_call(
        _moe_gateup_body, grid_spec=grid_spec,
        out_shape=jax.ShapeDtypeStruct((n_slots, f), BF16),
        compiler_params=_cparams(2), name="moe_gateup")(*meta, xs, wg, wu)


def _down_body(a_ref, w_ref, res_ref, o_ref, acc_ref):
    kk = pl.program_id(2)

    @pl.when(kk == 0)
    def _():
        acc_ref[...] = jnp.zeros_like(acc_ref)

    acc_ref[...] += jnp.dot(a_ref[...], w_ref[...].astype(BF16), preferred_element_type=F32)

    @pl.when(kk == pl.num_programs(2) - 1)
    def _():
        o_ref[...] = (acc_ref[...] + res_ref[...]).astype(o_ref.dtype)


def _down(a, w, res, *, tm=TM_TOK, tn=1024, tk=1024):
    m, k = a.shape
    n = w.shape[1]
    return pl.pallas_call(
        _down_body, grid=(m // tm, n // tn, k // tk),
        in_specs=[pl.BlockSpec((tm, tk), lambda i, j, kk: (i, kk)),
                  pl.BlockSpec((tk, tn), lambda i, j, kk: (kk, j)),
                  pl.BlockSpec((tm, tn), lambda i, j, kk: (i, j))],
        out_specs=pl.BlockSpec((tm, tn), lambda i, j, kk: (i, j)),
        out_shape=jax.ShapeDtypeStruct((m, n), F32),
        scratch_shapes=[pltpu.VMEM((tm, tn), F32)],
        compiler_params=_cparams(3), name="ffn_down")(a, w, res)


def _moe_down_body(te_ref, src_ref, first_ref, valid_ref, a_ref, w_ref, o_ref, acc_ref):
    r = pl.program_id(0)
    kk = pl.program_id(2)

    @pl.when(valid_ref[r] == 1)
    def _():
        @pl.when(kk == 0)
        def _():
            acc_ref[...] = jnp.zeros_like(acc_ref)

        acc_ref[...] += jnp.dot(a_ref[...], w_ref[...].astype(BF16),
                                preferred_element_type=F32)

        @pl.when(kk == pl.num_programs(2) - 1)
        def _():
            o_ref[...] = acc_ref[...].astype(o_ref.dtype)


def _moe_down(meta, a, w, *, tm=TM_MOE, tn=1024, tk=1024):
    n_slots, k = a.shape
    n = w.shape[2]
    nn, nk = n // tn, k // tk

    def a_map(r, j, kk, te, src, fi, va):
        return (src[r], jnp.where(va[r] == 1, kk, nk - 1))

    def w_map(r, j, kk, te, src, fi, va):
        live = va[r] == 1
        return (te[r], jnp.where(live, kk, nk - 1), jnp.where(live, j, nn - 1))

    def o_map(r, j, kk, te, src, fi, va):
        return (src[r], jnp.where(va[r] == 1, j, nn - 1))

    grid_spec = pltpu.PrefetchScalarGridSpec(
        num_scalar_prefetch=4, grid=(n_slots // tm, nn, nk),
        in_specs=[pl.BlockSpec((tm, tk), a_map), pl.BlockSpec((None, tk, tn), w_map)],
        out_specs=pl.BlockSpec((tm, tn), o_map),
        scratch_shapes=[pltpu.VMEM((tm, tn), F32)])
    return pl.pallas_call(
        _moe_down_body, grid_spec=grid_spec,
        out_shape=jax.ShapeDtypeStruct((n_slots, n), BF16),
        compiler_params=_cparams(3), name="moe_down")(*meta, a, w)


def _gla_body(q_ref, k_ref, v_ref, g_ref, a_ref, wa_ref, ba_ref, gn_ref, s0_ref,
              o_ref, s_ref):
    c = pl.program_id(1)

    @pl.when(c == 0)
    def _():
        s_ref[...] = jnp.zeros_like(s_ref)

    @pl.when(c >= N_PROMPT_CHUNKS)
    def _():
        s_ref[...] = s0_ref[...]

    q = q_ref[...].astype(F32) * (GLA_DK ** -0.5)
    k = k_ref[...].astype(F32)
    v = v_ref[...]
    g = g_ref[...].astype(F32)

    xg = jnp.dot(a_ref[...].astype(BF16), wa_ref[...].astype(BF16),
                 preferred_element_type=F32) + ba_ref[...]
    la = (jnp.minimum(xg, 0.0) - jnp.log1p(jnp.exp(-jnp.abs(xg)))) * (1.0 / GLA_GATE_NORM)

    row = lax.broadcasted_iota(jnp.int32, (CHUNK, CHUNK), 0)
    col = lax.broadcasted_iota(jnp.int32, (CHUNK, CHUNK), 1)
    causal = row >= col
    tri = causal.astype(BF16)
    la_hi = la.astype(BF16)
    la_lo = (la - la_hi.astype(F32)).astype(BF16)
    cum = (jnp.dot(tri, la_hi, preferred_element_type=F32)
           + jnp.dot(tri, la_lo, preferred_element_type=F32))
    last = cum[CHUNK - 1:CHUNK, :]

    q_in = q * jnp.exp(cum)
    k_out = k * jnp.exp(last - cum)
    q_loc = q_in * jnp.exp(-last)

    s_old = s_ref[...]
    attn = lax.dot_general(q_loc.astype(BF16), k_out.astype(BF16),
                           (((1,), (1,)), ((), ())), preferred_element_type=F32)
    attn = jnp.where(causal, attn, 0.0)
    o = (jnp.dot(q_in.astype(BF16), s_old.astype(BF16), preferred_element_type=F32)
         + jnp.dot(attn.astype(BF16), v, preferred_element_type=F32))

    kv = lax.dot_general(k_out.astype(BF16), v, (((0,), (0,)), ((), ())),
                         preferred_element_type=F32)
    dec_col = jnp.exp(jnp.transpose(jnp.broadcast_to(last, (LANE, GLA_DK))))
    dec = jnp.concatenate([dec_col] * (GLA_DV // LANE), axis=1)
    s_ref[...] = dec * s_old + kv

    on = o * lax.rsqrt(jnp.mean(o * o, axis=-1, keepdims=True) + RMS_EPS) * gn_ref[...]
    o_ref[...] = (on * (g * jax.nn.sigmoid(g))).astype(o_ref.dtype)


def _gla(proj, a_low, w_a2p, b_a, g_norm, s0):
    hq = QK_A // GLA_DK
    hv = (2 * QK_A) // GLA_DV
    n_seq = 1 + DEC_BATCH

    def s0_map(h, c):
        return (jnp.maximum(c - N_PROMPT_CHUNKS, 0), h, 0, 0)

    def s_map(h, c):
        return (jnp.where(c < N_PROMPT_CHUNKS, 0, c - N_PROMPT_CHUNKS + 1), h, 0, 0)

    return pl.pallas_call(
        _gla_body, grid=(GLA_HEADS, N_CHUNKS),
        in_specs=[pl.BlockSpec((CHUNK, GLA_DK), lambda h, c: (c, h)),
                  pl.BlockSpec((CHUNK, GLA_DK), lambda h, c: (c, hq + h)),
                  pl.BlockSpec((CHUNK, GLA_DV), lambda h, c: (c, hv + h)),
                  pl.BlockSpec((CHUNK, GLA_DV), lambda h, c: (c, hv + GLA_HEADS + h)),
                  pl.BlockSpec((CHUNK, LANE), lambda h, c: (c, 0)),
                  pl.BlockSpec((LANE, GLA_DK), lambda h, c: (0, h)),
                  pl.BlockSpec((1, GLA_DK), lambda h, c: (0, h)),
                  pl.BlockSpec((1, GLA_DV), lambda h, c: (0, 0)),
                  pl.BlockSpec((None, None, GLA_DK, GLA_DV), s0_map)],
        out_specs=[pl.BlockSpec((CHUNK, GLA_DV), lambda h, c: (c, h)),
                   pl.BlockSpec((None, None, GLA_DK, GLA_DV), s_map)],
        out_shape=[jax.ShapeDtypeStruct((T_TOK, V_A), BF16),
                   jax.ShapeDtypeStruct((n_seq, GLA_HEADS, GLA_DK, GLA_DV), F32)],
        compiler_params=_cparams(2), name="gla")(
            proj, proj, proj, proj, a_low, w_a2p, b_a, g_norm, s0)


def _swa_body(q_ref, k0_ref, k1_ref, k2_ref, v0_ref, v1_ref, v2_ref, sink_ref, o_ref):
    c = pl.program_id(0)
    q = q_ref[...].astype(F32)
    kcat = jnp.concatenate([k0_ref[...], k1_ref[...], k2_ref[...]], axis=0)
    vcat = jnp.concatenate([v0_ref[...], v1_ref[...], v2_ref[...]], axis=0)
    n_keys = (WIN_CHUNKS + 1) * CHUNK
    key_chunk = lax.broadcasted_iota(jnp.int32, (1, n_keys), 1) // CHUNK
    valid = jnp.logical_or(c >= N_PROMPT_CHUNKS, c - WIN_CHUNKS + key_chunk >= 0)
    hd = SWA_HEAD_DIM
    outs = []
    for kh in range(SWA_KV_HEADS):
        qg = jnp.concatenate(
            [q[:, (kh * SWA_GROUP + g) * hd:(kh * SWA_GROUP + g + 1) * hd]
             for g in range(SWA_GROUP)], axis=0)
        kk = kcat[:, kh * hd:(kh + 1) * hd].astype(BF16)
        vv = vcat[:, kh * hd:(kh + 1) * hd].astype(BF16)
        s = lax.dot_general(qg.astype(BF16), kk, (((1,), (1,)), ((), ())),
                            preferred_element_type=F32) * (hd ** -0.5)
        s = jnp.where(valid, s, -1e30)
        sink = sink_ref[kh][:, 0:1]
        m = jnp.maximum(jnp.max(s, axis=-1, keepdims=True), sink)
        p = jnp.exp(s - m)
        denom = jnp.sum(p, axis=-1, keepdims=True) + jnp.exp(sink - m)
        o = jnp.dot(p.astype(BF16), vv, preferred_element_type=F32) / denom
        outs.append(jnp.concatenate(
            [o[g * CHUNK:(g + 1) * CHUNK, :] for g in range(SWA_GROUP)], axis=1))
    o_ref[...] = jnp.concatenate(outs, axis=1).astype(o_ref.dtype)


def _swa(q, kf, vf, sink_rows):
    def kv_map(j):
        def index(c):
            prompt = jnp.maximum(c - WIN_CHUNKS + j, 0)
            sample = N_PROMPT_CHUNKS + (WIN_CHUNKS + 1) * (c - N_PROMPT_CHUNKS) + j
            return (jnp.where(c < N_PROMPT_CHUNKS, prompt, sample), 0)
        return index

    kv_specs = [pl.BlockSpec((CHUNK, KV_B), kv_map(j)) for j in range(WIN_CHUNKS + 1)]
    return pl.pallas_call(
        _swa_body, grid=(N_CHUNKS,),
        in_specs=[pl.BlockSpec((CHUNK, D_MODEL), lambda c: (c, 0))] + kv_specs + kv_specs
        + [pl.BlockSpec(sink_rows.shape, lambda c: (0, 0, 0))],
        out_specs=pl.BlockSpec((CHUNK, D_MODEL), lambda c: (c, 0)),
        out_shape=jax.ShapeDtypeStruct((T_TOK, D_MODEL), BF16),
        compiler_params=_cparams(1), name="swa")(q, kf, kf, kf, vf, vf, vf, sink_rows)


def _route(logits, tm):
    n_tok = logits.shape[0]
    top_val, top_idx = lax.top_k(logits[:, :N_EXPERTS], TOP_K)
    gates = jax.nn.softmax(top_val, axis=-1)
    flat_e = top_idx.reshape(-1).astype(jnp.int32)
    n_assign = n_tok * TOP_K
    onehot = (flat_e[:, None] == jnp.arange(N_EXPERTS, dtype=jnp.int32)[None, :]).astype(jnp.int32)
    csum = jnp.cumsum(onehot, axis=0)
    rank = jnp.take_along_axis(csum, flat_e[:, None], axis=1)[:, 0] - 1
    counts = csum[-1]
    n_tiles_e = (counts + tm - 1) // tm
    tile_end = jnp.cumsum(n_tiles_e)
    tile_start = tile_end - n_tiles_e
    dest = tile_start[flat_e] * tm + rank
    n_tiles = -(-n_assign // tm) + N_EXPERTS
    n_slots = n_tiles * tm
    slot_tok = jnp.full((n_slots,), n_tok, jnp.int32).at[dest].set(
        jnp.arange(n_assign, dtype=jnp.int32) // TOP_K)
    n_used = tile_end[-1]
    r = jnp.arange(n_tiles, dtype=jnp.int32)
    valid = r < n_used
    src = jnp.minimum(r, n_used - 1)
    te = jnp.clip(jnp.searchsorted(tile_end, src, side='right'), 0, N_EXPERTS - 1).astype(jnp.int32)
    first = jnp.logical_and(valid, r == tile_start[te])
    meta = (te, src.astype(jnp.int32), first.astype(jnp.int32), valid.astype(jnp.int32))
    return meta, slot_tok, dest.reshape(n_tok, TOP_K), gates


def kernel(x_prompt, x_sample, state_gla, cache_swa_k, cache_swa_v, ln_mix, ln_ff, ln_kv, ln_out, gla_w_in, gla_w_a2, gla_b_a, gla_norm, gla_w_o, kv_w, kv_b, swa_w_q, swa_b_q, swa_sinks, swa_w_o, swa_b_o, ffd_w_gate, ffd_w_up, ffd_w_down, moe_router, moe_w_gate, moe_w_up, moe_w_down):
    x = jnp.concatenate([x_prompt.reshape(SEQ, D_MODEL),
                         x_sample.reshape(DEC_BATCH * DEC_SEQ, D_MODEL)], axis=0)

    (hn,) = _rmsnorm(x, ln_mix[0:1])
    w_in = gla_w_in[0]
    proj = _mm(hn, w_in, n_cols=GLA_MAIN, tn=512, out_dtype=BF16)
    w_low = jnp.pad(w_in[:, GLA_MAIN:], ((0, 0), (0, LANE - GLA_GATE_RANK)))
    a_low = _mm(hn, w_low, n_cols=LANE, tn=LANE, out_dtype=F32)
    w_a2p = jnp.pad(gla_w_a2[0], ((0, LANE - GLA_GATE_RANK), (0, 0)))
    o_gla, states = _gla(proj, a_low, w_a2p, gla_b_a[0:1], gla_norm[0:1], state_gla[0])
    h = _mm(o_gla, gla_w_o[0], n_cols=D_MODEL, tn=512, res=x, out_dtype=F32)

    (hn,) = _rmsnorm(h, ln_ff[0:1])
    act = _gateup(hn, ffd_w_gate[0], ffd_w_up[0])
    h = _down(act, ffd_w_down[0], h)

    hkv, hn = _rmsnorm(h, jnp.stack([ln_kv, ln_mix[1]]))
    pos = jnp.concatenate([jnp.arange(SEQ, dtype=jnp.int32),
                           jnp.tile(PAST_LEN + jnp.arange(DEC_SEQ, dtype=jnp.int32), DEC_BATCH)])
    half = SWA_HEAD_DIM // 2
    inv_freq = ROPE_THETA ** (-jnp.arange(half, dtype=F32) / half)
    ang = pos.astype(F32)[:, None] * inv_freq[None, :]
    cos, sin = jnp.cos(ang), jnp.sin(ang)
    rope = (jnp.concatenate([cos, cos, cos, cos], axis=1),
            jnp.concatenate([-sin, sin, -sin, sin], axis=1))
    kv_bias = kv_b[None, :]
    k_new = _mm(hkv, kv_w, n_cols=KV_B, tn=512, bias=kv_bias, rope=rope, out_dtype=F32)
    v_new = _mm(hkv, kv_w, n_cols=KV_B, tn=512, col_off=KV_B, bias=kv_bias, out_dtype=F32)
    q = _mm(hn, swa_w_q[0], n_cols=D_MODEL, tn=512, bias=swa_b_q[0:1], rope=rope,
            out_dtype=BF16)

    cache_rows = cache_swa_k.shape[1]

    def with_cache(new, cache):
        samp = jnp.concatenate([cache.reshape(DEC_BATCH, cache_rows, KV_B),
                                new[SEQ:].reshape(DEC_BATCH, DEC_SEQ, KV_B)], axis=1)
        return samp

    k_samp = with_cache(k_new, cache_swa_k)
    v_samp = with_cache(v_new, cache_swa_v)
    kf = jnp.concatenate([k_new[:SEQ], k_samp.reshape(-1, KV_B)], axis=0)
    vf = jnp.concatenate([v_new[:SEQ], v_samp.reshape(-1, KV_B)], axis=0)
    sink_rows = jnp.broadcast_to(
        jnp.repeat(swa_sinks[0], CHUNK).reshape(SWA_KV_HEADS, SWA_GROUP * CHUNK, 1),
        (SWA_KV_HEADS, SWA_GROUP * CHUNK, LANE))
    o_swa = _swa(q, kf, vf, sink_rows)
    h = _mm(o_swa, swa_w_o[0], n_cols=D_MODEL, tn=512, bias=swa_b_o[0:1], res=h, out_dtype=F32)

    router = jnp.pad(moe_router[0], ((0, 0), (0, LANE - N_EXPERTS)))
    hn, logits = _rmsnorm(h, ln_ff[1:2], router=router)
    meta, slot_tok, dest, gates = _route(logits, TM_MOE)
    xs = jnp.concatenate([hn, jnp.zeros((1, D_MODEL), hn.dtype)], axis=0)[slot_tok]
    act = _moe_gateup(meta, xs, moe_w_gate[0], moe_w_up[0])
    yb = _moe_down(meta, act, moe_w_down[0])
    ff = (yb[dest[:, 0]].astype(F32) * gates[:, 0:1]
          + yb[dest[:, 1]].astype(F32) * gates[:, 1:2])

    (y_prompt,) = _rmsnorm(h, ln_out[None, :], delta=ff, out_dtype=F32, n_rows=SEQ)
    (y_sample,) = _rmsnorm(h, ln_out[None, :], delta=ff, out_dtype=F32, row_off=SEQ,
                           n_rows=DEC_BATCH * DEC_SEQ)

    kv_shape = (SWA_KV_HEADS, SWA_HEAD_DIM)
    return (y_prompt.reshape(1, SEQ, D_MODEL),
            y_sample.reshape(DEC_BATCH, DEC_SEQ, D_MODEL),
            states[0:1][None],
            k_new[SEQ - WINDOW:SEQ].reshape(1, WINDOW, *kv_shape),
            v_new[SEQ - WINDOW:SEQ].reshape(1, WINDOW, *kv_shape),
            states[1:][None],
            k_samp[:, -cache_rows:].reshape(DEC_BATCH, cache_rows, *kv_shape),
            v_samp[:, -cache_rows:].reshape(DEC_BATCH, cache_rows, *kv_shape))
```

```python
import functools

import jax
import jax.numpy as jnp
from jax import lax
from jax.experimental import pallas as pl
from jax.experimental.pallas import tpu as pltpu

F32 = jnp.float32
BF16 = jnp.bfloat16

D_MODEL = 4096
SEQ = 8192
DEC_BATCH = 8
DEC_SEQ = 64
PAST_LEN = 1024
CHUNK = 64
RMS_EPS = 1e-6
ROPE_THETA = 10000.0

GLA_HEADS = 8
GLA_DK = 256
GLA_DV = 512
GLA_GATE_RANK = 16
GLA_GATE_NORM = 16.0
QK_A = GLA_HEADS * GLA_DK
V_A = GLA_HEADS * GLA_DV
GLA_MAIN = 2 * QK_A + 2 * V_A

SWA_HEAD_DIM = 64
SWA_HEADS = 64
SWA_KV_HEADS = 8
SWA_GROUP = SWA_HEADS // SWA_KV_HEADS
WINDOW = 128
WIN_CHUNKS = WINDOW // CHUNK
KV_B = SWA_KV_HEADS * SWA_HEAD_DIM

FF_DIM = 14336
N_EXPERTS = 8
TOP_K = 2

T_TOK = SEQ + DEC_BATCH * DEC_SEQ
N_PROMPT_CHUNKS = SEQ // CHUNK
N_CHUNKS = T_TOK // CHUNK
LANE = 128

VMEM_LIMIT = 56 * 1024 * 1024

TM_TOK = 1088
MOE_SUB = 544
MOE_TILE = 2 * MOE_SUB


def _cparams(n_axes):
    return pltpu.CompilerParams(dimension_semantics=("arbitrary",) * n_axes,
                                vmem_limit_bytes=VMEM_LIMIT)


def _top2_route(logits):
    lane = lax.broadcasted_iota(jnp.int32, logits.shape, 1).astype(F32)
    neg = jnp.float32(-jnp.inf)
    lg = jnp.where(lane < N_EXPERTS, logits, neg)
    m1 = jnp.max(lg, axis=-1, keepdims=True)
    i1 = jnp.min(jnp.where(lg == m1, lane, float(LANE)), axis=-1, keepdims=True)
    lg2 = jnp.where(lane == i1, neg, lg)
    m2 = jnp.max(lg2, axis=-1, keepdims=True)
    i2 = jnp.min(jnp.where(lg2 == m2, lane, float(LANE)), axis=-1, keepdims=True)
    e2 = jnp.exp(m2 - m1)
    inv = 1.0 / (1.0 + e2)
    return jnp.where(lane == 0, i1,
                     jnp.where(lane == 1, i2, jnp.where(lane == 2, inv, e2 * inv)))


def _rmsnorm_body(*refs, n_out, has_delta, emit_sum, has_router):
    it = iter(refs)
    x_ref = next(it)
    d_ref = next(it) if has_delta else None
    g_ref = next(it)
    r_ref = next(it) if has_router else None
    sum_ref = next(it) if emit_sum else None
    o_refs = [next(it) for _ in range(n_out)]
    rt_ref = next(it) if has_router else None

    x = x_ref[...]
    if has_delta:
        x = x + d_ref[...].astype(F32)
    if emit_sum:
        sum_ref[...] = x
    y = x * lax.rsqrt(jnp.mean(x * x, axis=-1, keepdims=True) + RMS_EPS)
    for i, o_ref in enumerate(o_refs):
        yi = y * g_ref[i:i + 1, :]
        o_ref[...] = yi.astype(o_ref.dtype)
        if has_router and i == 0:
            logits = jnp.dot(yi, r_ref[...], precision=lax.Precision.HIGHEST,
                             preferred_element_type=F32)
            rt_ref[...] = _top2_route(logits)


def _rmsnorm(x, gains, *, delta=None, emit_sum=False, router=None, out_dtype=BF16,
             row_off=0, n_rows=None, tr=256):
    n_rows = x.shape[0] if n_rows is None else n_rows
    d = x.shape[1]
    n_out = gains.shape[0]
    off = row_off // tr
    row_spec = pl.BlockSpec((tr, d), lambda i: (i + off, 0))
    out_spec = pl.BlockSpec((tr, d), lambda i: (i, 0))
    in_specs = [row_spec]
    args = [x]
    if delta is not None:
        in_specs.append(row_spec)
        args.append(delta)
    in_specs.append(pl.BlockSpec((n_out, d), lambda i: (0, 0)))
    args.append(gains)
    if router is not None:
        in_specs.append(pl.BlockSpec(router.shape, lambda i: (0, 0)))
        args.append(router)
    out_shape, out_specs = [], []
    if emit_sum:
        out_shape.append(jax.ShapeDtypeStruct((n_rows, d), F32))
        out_specs.append(out_spec)
    for _ in range(n_out):
        out_shape.append(jax.ShapeDtypeStruct((n_rows, d), out_dtype))
        out_specs.append(out_spec)
    if router is not None:
        out_shape.append(jax.ShapeDtypeStruct((n_rows, router.shape[1]), F32))
        out_specs.append(pl.BlockSpec((tr, router.shape[1]), lambda i: (i, 0)))
    body = functools.partial(_rmsnorm_body, n_out=n_out, has_delta=delta is not None,
                             emit_sum=emit_sum, has_router=router is not None)
    return pl.pallas_call(
        body, grid=(n_rows // tr,), in_specs=in_specs, out_specs=out_specs,
        out_shape=out_shape, compiler_params=_cparams(1), name="rmsnorm")(*args)


def _rope_tile(acc, cos_ref, sin_ref):
    cos = cos_ref[...]
    sin = sin_ref[...]
    lane = lax.broadcasted_iota(jnp.int32, (acc.shape[0], LANE), 1)
    first_half = (lane % SWA_HEAD_DIM) < (SWA_HEAD_DIM // 2)
    outs = []
    for j in range(acc.shape[1] // LANE):
        blk = acc[:, j * LANE:(j + 1) * LANE]
        rot = jnp.where(first_half,
                        pltpu.roll(blk, LANE - SWA_HEAD_DIM // 2, 1),
                        pltpu.roll(blk, SWA_HEAD_DIM // 2, 1))
        outs.append(blk * cos + rot * sin)
    return jnp.concatenate(outs, axis=1)


def _mm_body(*refs, has_bias, has_res, has_rope):
    it = iter(refs)
    x_ref = next(it)
    w_ref = next(it)
    b_ref = next(it) if has_bias else None
    cos_ref = next(it) if has_rope else None
    sin_ref = next(it) if has_rope else None
    res_ref = next(it) if has_res else None
    o_ref = next(it)
    wbf_ref = next(it)

    @pl.when(pl.program_id(1) == 0)
    def _():
        wbf_ref[...] = w_ref[...].astype(BF16)

    acc = jnp.dot(x_ref[...], wbf_ref[...], preferred_element_type=F32)
    if has_bias:
        acc = acc + b_ref[...]
    if has_rope:
        acc = _rope_tile(acc, cos_ref, sin_ref)
    if has_res:
        acc = acc + res_ref[...]
    o_ref[...] = acc.astype(o_ref.dtype)


def _mm(x, w, *, n_cols, tn, tm=TM_TOK, col_off=0, bias=None, res=None, rope=None,
        out_dtype=F32):
    m, k = x.shape
    cb = col_off // tn
    in_specs = [pl.BlockSpec((tm, k), lambda n, i: (i, 0)),
                pl.BlockSpec((k, tn), lambda n, i: (0, n + cb))]
    args = [x, w]
    if bias is not None:
        in_specs.append(pl.BlockSpec((1, tn), lambda n, i: (0, n + cb)))
        args.append(bias)
    if rope is not None:
        for t in rope:
            in_specs.append(pl.BlockSpec((tm, LANE), lambda n, i: (i, 0)))
            args.append(t)
    if res is not None:
        in_specs.append(pl.BlockSpec((tm, tn), lambda n, i: (i, n)))
        args.append(res)
    body = functools.partial(_mm_body, has_bias=bias is not None, has_res=res is not None,
                             has_rope=rope is not None)
    return pl.pallas_call(
        body, grid=(n_cols // tn, m // tm), in_specs=in_specs,
        out_specs=pl.BlockSpec((tm, tn), lambda n, i: (i, n)),
        out_shape=jax.ShapeDtypeStruct((m, n_cols), out_dtype),
        scratch_shapes=[pltpu.VMEM((k, tn), BF16)],
        compiler_params=_cparams(2), name="matmul_ws")(*args)


def _swiglu_tile(x, wg, wu):
    g = jnp.dot(x, wg, preferred_element_type=F32)
    u = jnp.dot(x, wu, preferred_element_type=F32)
    return g * jax.nn.sigmoid(g) * u


def _gateup_body(x_ref, wg_ref, wu_ref, o_ref, wgbf_ref, wubf_ref):
    @pl.when(pl.program_id(1) == 0)
    def _():
        wgbf_ref[...] = wg_ref[...].astype(BF16)
        wubf_ref[...] = wu_ref[...].astype(BF16)

    o_ref[...] = _swiglu_tile(x_ref[...], wgbf_ref[...], wubf_ref[...]).astype(o_ref.dtype)


def _gateup(x, wg, wu, *, tm=TM_TOK, tn=256):
    m, k = x.shape
    f = wg.shape[1]
    return pl.pallas_call(
        _gateup_body, grid=(f // tn, m // tm),
        in_specs=[pl.BlockSpec((tm, k), lambda n, i: (i, 0)),
                  pl.BlockSpec((k, tn), lambda n, i: (0, n)),
                  pl.BlockSpec((k, tn), lambda n, i: (0, n))],
        out_specs=pl.BlockSpec((tm, tn), lambda n, i: (i, n)),
        out_shape=jax.ShapeDtypeStruct((m, f), BF16),
        scratch_shapes=[pltpu.VMEM((k, tn), BF16), pltpu.VMEM((k, tn), BF16)],
        compiler_params=_cparams(2), name="ffn_gateup")(x, wg, wu)


def _moe_gateup_body(te_ref, src_ref, nsub_ref, x_ref, wg_ref, wu_ref, o_ref,
                     wgbf_ref, wubf_ref):
    r = pl.program_id(0)
    nsub = nsub_ref[r]

    @pl.when(nsub > 0)
    def _():
        wgbf_ref[...] = wg_ref[...].astype(BF16)
        wubf_ref[...] = wu_ref[...].astype(BF16)
        for j in range(MOE_TILE // MOE_SUB):
            rows = pl.ds(j * MOE_SUB, MOE_SUB)

            @pl.when(j < nsub)
            def _():
                o_ref[rows, :] = _swiglu_tile(x_ref[rows, :], wgbf_ref[...],
                                              wubf_ref[...]).astype(o_ref.dtype)

            @pl.when(j >= nsub)
            def _():
                o_ref[rows, :] = jnp.zeros((MOE_SUB, o_ref.shape[1]), o_ref.dtype)


def _moe_gateup(meta, xs, wg, wu, *, tn=256):
    n_slots, k = xs.shape
    f = wg.shape[2]
    nn = f // tn

    def col(n, r, ns):
        return jnp.where(ns[r] > 0, n, nn - 1)

    grid_spec = pltpu.PrefetchScalarGridSpec(
        num_scalar_prefetch=3, grid=(n_slots // MOE_TILE, nn),
        in_specs=[pl.BlockSpec((MOE_TILE, k), lambda r, n, te, src, ns: (src[r], 0)),
                  pl.BlockSpec((None, k, tn), lambda r, n, te, src, ns: (te[r], 0, col(n, r, ns))),
                  pl.BlockSpec((None, k, tn), lambda r, n, te, src, ns: (te[r], 0, col(n, r, ns)))],
        out_specs=pl.BlockSpec((MOE_TILE, tn), lambda r, n, te, src, ns: (src[r], col(n, r, ns))),
        scratch_shapes=[pltpu.VMEM((k, tn), BF16), pltpu.VMEM((k, tn), BF16)])
    return pl.pallas_call(
        _moe_gateup_body, grid_spec=grid_spec,
        out_shape=jax.ShapeDtypeStruct((n_slots, f), BF16),
        compiler_params=_cparams(2), name="moe_gateup")(*meta, xs, wg, wu)


def _down_body(a_ref, w_ref, res_ref, o_ref, acc_ref):
    kk = pl.program_id(2)

    @pl.when(kk == 0)
    def _():
        acc_ref[...] = jnp.zeros_like(acc_ref)

    acc_ref[...] += jnp.dot(a_ref[...], w_ref[...].astype(BF16), preferred_element_type=F32)

    @pl.when(kk == pl.num_programs(2) - 1)
    def _():
        o_ref[...] = (acc_ref[...] + res_ref[...]).astype(o_ref.dtype)


def _down(a, w, res, *, tm=TM_TOK, tn=1024, tk=1024):
    m, k = a.shape
    n = w.shape[1]
    return pl.pallas_call(
        _down_body, grid=(m // tm, n // tn, k // tk),
        in_specs=[pl.BlockSpec((tm, tk), lambda i, j, kk: (i, kk)),
                  pl.BlockSpec((tk, tn), lambda i, j, kk: (kk, j)),
                  pl.BlockSpec((tm, tn), lambda i, j, kk: (i, j))],
        out_specs=pl.BlockSpec((tm, tn), lambda i, j, kk: (i, j)),
        out_shape=jax.ShapeDtypeStruct((m, n), F32),
        scratch_shapes=[pltpu.VMEM((tm, tn), F32)],
        compiler_params=_cparams(3), name="ffn_down")(a, w, res)


def _moe_down_body(te_ref, src_ref, nsub_ref, a_ref, w_ref, o_ref, acc_ref, wbf_ref):
    r = pl.program_id(0)
    kk = pl.program_id(2)
    nsub = nsub_ref[r]

    @pl.when(nsub > 0)
    def _():
        @pl.when(kk == 0)
        def _():
            acc_ref[...] = jnp.zeros_like(acc_ref)

        wbf_ref[...] = w_ref[...].astype(BF16)
        for j in range(MOE_TILE // MOE_SUB):
            rows = pl.ds(j * MOE_SUB, MOE_SUB)

            @pl.when(j < nsub)
            def _():
                acc_ref[rows, :] += jnp.dot(a_ref[rows, :], wbf_ref[...],
                                            preferred_element_type=F32)

        @pl.when(kk == pl.num_programs(2) - 1)
        def _():
            o_ref[...] = acc_ref[...].astype(o_ref.dtype)


def _moe_down(meta, a, w, *, tn=1024, tk=1024):
    n_slots, k = a.shape
    n = w.shape[2]
    nn, nk = n // tn, k // tk

    def a_map(r, j, kk, te, src, ns):
        return (src[r], jnp.where(ns[r] > 0, kk, nk - 1))

    def w_map(r, j, kk, te, src, ns):
        live = ns[r] > 0
        return (te[r], jnp.where(live, kk, nk - 1), jnp.where(live, j, nn - 1))

    def o_map(r, j, kk, te, src, ns):
        return (src[r], jnp.where(ns[r] > 0, j, nn - 1))

    grid_spec = pltpu.PrefetchScalarGridSpec(
        num_scalar_prefetch=3, grid=(n_slots // MOE_TILE, nn, nk),
        in_specs=[pl.BlockSpec((MOE_TILE, tk), a_map), pl.BlockSpec((None, tk, tn), w_map)],
        out_specs=pl.BlockSpec((MOE_TILE, tn), o_map),
        scratch_shapes=[pltpu.VMEM((MOE_TILE, tn), F32), pltpu.VMEM((tk, tn), BF16)])
    return pl.pallas_call(
        _moe_down_body, grid_spec=grid_spec,
        out_shape=jax.ShapeDtypeStruct((n_slots, n), BF16),
        compiler_params=_cparams(3), name="moe_down")(*meta, a, w)


def _gla_body(q_ref, k_ref, v_ref, g_ref, a_ref, wa_ref, ba_ref, gn_ref, s0_ref,
              o_ref, s_ref):
    c = pl.program_id(0)

    @pl.when(c == 0)
    def _():
        s_ref[...] = jnp.zeros_like(s_ref)

    @pl.when(c >= N_PROMPT_CHUNKS)
    def _():
        s_ref[...] = s0_ref[...]

    xg = jnp.dot(a_ref[...].astype(BF16), wa_ref[...].astype(BF16),
                 preferred_element_type=F32) + ba_ref[...]
    la = (jnp.minimum(xg, 0.0) - jnp.log1p(jnp.exp(-jnp.abs(xg)))) * (1.0 / GLA_GATE_NORM)

    row = lax.broadcasted_iota(jnp.int32, (CHUNK, CHUNK), 0)
    col = lax.broadcasted_iota(jnp.int32, (CHUNK, CHUNK), 1)
    causal = row >= col
    tri = causal.astype(BF16)
    la_hi = la.astype(BF16)
    la_lo = (la - la_hi.astype(F32)).astype(BF16)
    cum = (jnp.dot(tri, la_hi, preferred_element_type=F32)
           + jnp.dot(tri, la_lo, preferred_element_type=F32))
    last = cum[CHUNK - 1:CHUNK, :]

    q = q_ref[...].astype(F32) * (GLA_DK ** -0.5)
    q_in = q * jnp.exp(cum)
    k_out = (k_ref[...].astype(F32) * jnp.exp(last - cum)).astype(BF16)
    q_loc = (q_in * jnp.exp(-last)).astype(BF16)
    q_in = q_in.astype(BF16)
    dec_rows = jnp.broadcast_to(jnp.exp(last), (LANE, QK_A))

    for h in range(GLA_HEADS):
        dk = slice(h * GLA_DK, (h + 1) * GLA_DK)
        dv = slice(h * GLA_DV, (h + 1) * GLA_DV)
        v = v_ref[:, dv]
        g = g_ref[:, dv].astype(F32)
        s_old = s_ref[h]
        attn = lax.dot_general(q_loc[:, dk], k_out[:, dk], (((1,), (1,)), ((), ())),
                               preferred_element_type=F32)
        attn = jnp.where(causal, attn, 0.0)
        o = (jnp.dot(q_in[:, dk], s_old.astype(BF16), preferred_element_type=F32)
             + jnp.dot(attn.astype(BF16), v, preferred_element_type=F32))
        kv = lax.dot_general(k_out[:, dk], v, (((0,), (0,)), ((), ())),
                             preferred_element_type=F32)
        dec_col = jnp.transpose(dec_rows[:, dk])
        dec = jnp.concatenate([dec_col] * (GLA_DV // LANE), axis=1)
        s_ref[h] = dec * s_old + kv
        on = o * lax.rsqrt(jnp.mean(o * o, axis=-1, keepdims=True) + RMS_EPS) * gn_ref[...]
        o_ref[:, dv] = (on * (g * jax.nn.sigmoid(g))).astype(o_ref.dtype)


def _gla(proj, a_low, w_a2p, b_a, g_norm, s0):
    n_seq = 1 + DEC_BATCH
    state_block = (None, GLA_HEADS, GLA_DK, GLA_DV)

    def s0_map(c):
        return (jnp.maximum(c - N_PROMPT_CHUNKS, 0), 0, 0, 0)

    def s_map(c):
        return (jnp.where(c < N_PROMPT_CHUNKS, 0, c - N_PROMPT_CHUNKS + 1), 0, 0, 0)

    return pl.pallas_call(
        _gla_body, grid=(N_CHUNKS,),
        in_specs=[pl.BlockSpec((CHUNK, QK_A), lambda c: (c, 0)),
                  pl.BlockSpec((CHUNK, QK_A), lambda c: (c, 1)),
                  pl.BlockSpec((CHUNK, V_A), lambda c: (c, 1)),
                  pl.BlockSpec((CHUNK, V_A), lambda c: (c, 2)),
                  pl.BlockSpec((CHUNK, LANE), lambda c: (c, 0)),
                  pl.BlockSpec((LANE, QK_A), lambda c: (0, 0)),
                  pl.BlockSpec((1, QK_A), lambda c: (0, 0)),
                  pl.BlockSpec((1, GLA_DV), lambda c: (0, 0)),
                  pl.BlockSpec(state_block, s0_map)],
        out_specs=[pl.BlockSpec((CHUNK, V_A), lambda c: (c, 0)),
                   pl.BlockSpec(state_block, s_map)],
        out_shape=[jax.ShapeDtypeStruct((T_TOK, V_A), BF16),
                   jax.ShapeDtypeStruct((n_seq, GLA_HEADS, GLA_DK, GLA_DV), F32)],
        compiler_params=_cparams(1), name="gla")(
            proj, proj, proj, proj, a_low, w_a2p, b_a, g_norm, s0)


def _swa_body(q_ref, k0_ref, k1_ref, k2_ref, v0_ref, v1_ref, v2_ref, sink_ref, o_ref):
    c = pl.program_id(0)
    kcat = jnp.concatenate([k0_ref[...], k1_ref[...], k2_ref[...]], axis=0)
    vcat = jnp.concatenate([v0_ref[...], v1_ref[...], v2_ref[...]], axis=0)
    n_keys = (WIN_CHUNKS + 1) * CHUNK
    n_pairs = SWA_GROUP // 2
    n_q = n_pairs * CHUNK
    key_chunk = lax.broadcasted_iota(jnp.int32, (n_keys, n_q), 0) // CHUNK
    valid = jnp.logical_or(c >= N_PROMPT_CHUNKS, c - WIN_CHUNKS + key_chunk >= 0)
    low_half = lax.broadcasted_iota(jnp.int32, (n_keys, LANE), 1) < SWA_HEAD_DIM
    scale = SWA_HEAD_DIM ** -0.5

    for slab in range(KV_B // LANE):
        ks = kcat[:, slab * LANE:(slab + 1) * LANE]
        vs = vcat[:, slab * LANE:(slab + 1) * LANE]
        ks_sw = pltpu.roll(ks, SWA_HEAD_DIM, 1)
        vs_sw = pltpu.roll(vs, SWA_HEAD_DIM, 1)
        for par in range(2):
            kh = 2 * slab + par
            k_lo, k_hi = (ks, ks_sw) if par == 0 else (ks_sw, ks)
            v_lo, v_hi = (vs, vs_sw) if par == 0 else (vs_sw, vs)
            k_half = (jnp.where(low_half, k_lo, 0.0).astype(BF16),
                      jnp.where(low_half, 0.0, k_hi).astype(BF16))
            v_half = (jnp.where(low_half, v_lo, 0.0).astype(BF16),
                      jnp.where(low_half, 0.0, v_hi).astype(BF16))
            first = kh * n_pairs
            qs = jnp.concatenate([q_ref[:, (first + p) * LANE:(first + p + 1) * LANE]
                                  for p in range(n_pairs)], axis=0)
            acc = None
            for parity in range(2):
                s = lax.dot_general(k_half[parity], qs, (((1,), (1,)), ((), ())),
                                    preferred_element_type=F32) * scale
                s = jnp.where(valid, s, -1e30)
                sink = sink_ref[kh, parity][0:1, :]
                m = jnp.maximum(jnp.max(s, axis=0, keepdims=True), sink)
                p = jnp.exp(s - m)
                denom = jnp.sum(p, axis=0, keepdims=True) + jnp.exp(sink - m)
                pn = (p * (1.0 / denom)).astype(BF16)
                o = lax.dot_general(pn, v_half[parity], (((0,), (0,)), ((), ())),
                                    preferred_element_type=F32)
                acc = o if acc is None else acc + o
            for p in range(n_pairs):
                o_ref[:, (first + p) * LANE:(first + p + 1) * LANE] = (
                    acc[p * CHUNK:(p + 1) * CHUNK, :].astype(o_ref.dtype))


def _swa(q, kf, vf, sink_rows):
    def kv_map(j):
        def index(c):
            prompt = jnp.maximum(c - WIN_CHUNKS + j, 0)
            sample = N_PROMPT_CHUNKS + (WIN_CHUNKS + 1) * (c - N_PROMPT_CHUNKS) + j
            return (jnp.where(c < N_PROMPT_CHUNKS, prompt, sample), 0)
        return index

    kv_specs = [pl.BlockSpec((CHUNK, KV_B), kv_map(j)) for j in range(WIN_CHUNKS + 1)]
    return pl.pallas_call(
        _swa_body, grid=(N_CHUNKS,),
        in_specs=[pl.BlockSpec((CHUNK, D_MODEL), lambda c: (c, 0))] + kv_specs + kv_specs
        + [pl.BlockSpec(sink_rows.shape, lambda c: (0, 0, 0, 0))],
        out_specs=pl.BlockSpec((CHUNK, D_MODEL), lambda c: (c, 0)),
        out_shape=jax.ShapeDtypeStruct((T_TOK, D_MODEL), BF16),
        compiler_params=_cparams(1), name="swa")(q, kf, kf, kf, vf, vf, vf, sink_rows)


def _route(route):
    n_tok = route.shape[0]
    top_idx = route[:, 0:TOP_K].astype(jnp.int32)
    gates = route[:, TOP_K:2 * TOP_K]
    flat_e = top_idx.reshape(-1)
    n_assign = n_tok * TOP_K
    onehot = (flat_e[:, None] == jnp.arange(N_EXPERTS, dtype=jnp.int32)[None, :]).astype(jnp.int32)
    csum = jnp.cumsum(onehot, axis=0)
    rank = jnp.take_along_axis(csum, flat_e[:, None], axis=1)[:, 0] - 1
    counts = csum[-1]
    n_tiles_e = (counts + MOE_TILE - 1) // MOE_TILE
    tile_end = jnp.cumsum(n_tiles_e)
    tile_start = tile_end - n_tiles_e
    dest = tile_start[flat_e] * MOE_TILE + rank
    n_tiles = -(-n_assign // MOE_TILE) + N_EXPERTS
    n_slots = n_tiles * MOE_TILE
    slot_tok = jnp.full((n_slots,), n_tok, jnp.int32).at[dest].set(
        jnp.arange(n_assign, dtype=jnp.int32) // TOP_K)
    n_used = tile_end[-1]
    r = jnp.arange(n_tiles, dtype=jnp.int32)
    src = jnp.minimum(r, n_used - 1)
    te = jnp.clip(jnp.searchsorted(tile_end, src, side='right'), 0, N_EXPERTS - 1).astype(jnp.int32)
    rows_left = counts[te] - (r - tile_start[te]) * MOE_TILE
    nsub = jnp.where(r < n_used,
                     jnp.clip((rows_left + MOE_SUB - 1) // MOE_SUB, 0, MOE_TILE // MOE_SUB), 0)
    meta = (te, src.astype(jnp.int32), nsub.astype(jnp.int32))
    return meta, slot_tok, dest.reshape(n_tok, TOP_K), gates


def kernel(x_prompt, x_sample, state_gla, cache_swa_k, cache_swa_v, ln_mix, ln_ff, ln_kv, ln_out, gla_w_in, gla_w_a2, gla_b_a, gla_norm, gla_w_o, kv_w, kv_b, swa_w_q, swa_b_q, swa_sinks, swa_w_o, swa_b_o, ffd_w_gate, ffd_w_up, ffd_w_down, moe_router, moe_w_gate, moe_w_up, moe_w_down):
    x = jnp.concatenate([x_prompt.reshape(SEQ, D_MODEL),
                         x_sample.reshape(DEC_BATCH * DEC_SEQ, D_MODEL)], axis=0)

    (hn,) = _rmsnorm(x, ln_mix[0:1])
    w_in = gla_w_in[0]
    proj = _mm(hn, w_in, n_cols=GLA_MAIN, tn=512, out_dtype=BF16)
    w_low = jnp.pad(w_in[:, GLA_MAIN:], ((0, 0), (0, LANE - GLA_GATE_RANK)))
    a_low = _mm(hn, w_low, n_cols=LANE, tn=LANE, out_dtype=F32)
    w_a2p = jnp.pad(gla_w_a2[0], ((0, LANE - GLA_GATE_RANK), (0, 0)))
    o_gla, states = _gla(proj, a_low, w_a2p, gla_b_a[0:1], gla_norm[0:1], state_gla[0])
    h = _mm(o_gla, gla_w_o[0], n_cols=D_MODEL, tn=512, res=x, out_dtype=F32)

    (hn,) = _rmsnorm(h, ln_ff[0:1])
    act = _gateup(hn, ffd_w_gate[0], ffd_w_up[0])
    h = _down(act, ffd_w_down[0], h)

    hkv, hn = _rmsnorm(h, jnp.stack([ln_kv, ln_mix[1]]))
    pos = jnp.concatenate([jnp.arange(SEQ, dtype=jnp.int32),
                           jnp.tile(PAST_LEN + jnp.arange(DEC_SEQ, dtype=jnp.int32), DEC_BATCH)])
    half = SWA_HEAD_DIM // 2
    inv_freq = ROPE_THETA ** (-jnp.arange(half, dtype=F32) / half)
    ang = pos.astype(F32)[:, None] * inv_freq[None, :]
    cos, sin = jnp.cos(ang), jnp.sin(ang)
    rope = (jnp.concatenate([cos, cos, cos, cos], axis=1),
            jnp.concatenate([-sin, sin, -sin, sin], axis=1))
    kv_bias = kv_b[None, :]
    k_new = _mm(hkv, kv_w, n_cols=KV_B, tn=512, bias=kv_bias, rope=rope, out_dtype=F32)
    v_new = _mm(hkv, kv_w, n_cols=KV_B, tn=512, col_off=KV_B, bias=kv_bias, out_dtype=F32)
    q = _mm(hn, swa_w_q[0], n_cols=D_MODEL, tn=512, bias=swa_b_q[0:1], rope=rope,
            out_dtype=BF16)

    cache_rows = cache_swa_k.shape[1]

    def with_cache(new, cache):
        return jnp.concatenate([cache.reshape(DEC_BATCH, cache_rows, KV_B),
                                new[SEQ:].reshape(DEC_BATCH, DEC_SEQ, KV_B)], axis=1)

    k_samp = with_cache(k_new, cache_swa_k)
    v_samp = with_cache(v_new, cache_swa_v)
    kf = jnp.concatenate([k_new[:SEQ], k_samp.reshape(-1, KV_B)], axis=0)
    vf = jnp.concatenate([v_new[:SEQ], v_samp.reshape(-1, KV_B)], axis=0)
    sinks = swa_sinks[0].reshape(SWA_KV_HEADS, SWA_GROUP // 2, 2).transpose(0, 2, 1)
    sink_rows = jnp.broadcast_to(
        jnp.repeat(sinks, CHUNK, axis=2)[:, :, None, :],
        (SWA_KV_HEADS, 2, 8, (SWA_GROUP // 2) * CHUNK))
    o_swa = _swa(q, kf, vf, sink_rows)
    h = _mm(o_swa, swa_w_o[0], n_cols=D_MODEL, tn=512, bias=swa_b_o[0:1], res=h, out_dtype=F32)

    router = jnp.pad(moe_router[0], ((0, 0), (0, LANE - N_EXPERTS)))
    hn, route = _rmsnorm(h, ln_ff[1:2], router=router)
    meta, slot_tok, dest, gates = _route(route)
    xs = jnp.concatenate([hn, jnp.zeros((1, D_MODEL), hn.dtype)], axis=0)[slot_tok]
    act = _moe_gateup(meta, xs, moe_w_gate[0], moe_w_up[0])
    yb = _moe_down(meta, act, moe_w_down[0])
    ff = (yb[dest[:, 0]].astype(F32) * gates[:, 0:1]
          + yb[dest[:, 1]].astype(F32) * gates[:, 1:2])

    (y_prompt,) = _rmsnorm(h, ln_out[None, :], delta=ff, out_dtype=F32, n_rows=SEQ)
    (y_sample,) = _rmsnorm(h, ln_out[None, :], delta=ff, out_dtype=F32, row_off=SEQ,
                           n_rows=DEC_BATCH * DEC_SEQ)

    kv_shape = (SWA_KV_HEADS, SWA_HEAD_DIM)
    return (y_prompt.reshape(1, SEQ, D_MODEL),
            y_sample.reshape(DEC_BATCH, DEC_SEQ, D_MODEL),
            states[0:1][None],
            k_new[SEQ - WINDOW:SEQ].reshape(1, WINDOW, *kv_shape),
            v_new[SEQ - WINDOW:SEQ].reshape(1, WINDOW, *kv_shape),
            states[1:][None],
            k_samp[:, -cache_rows:].reshape(DEC_BATCH, cache_rows, *kv_shape),
            v_samp[:, -cache_rows:].reshape(DEC_BATCH, cache_rows, *kv_shape))
```

```python
import functools

import jax
import jax.numpy as jnp
from jax import lax
from jax.experimental import pallas as pl
from jax.experimental.pallas import tpu as pltpu

F32 = jnp.float32
BF16 = jnp.bfloat16

D_MODEL = 4096
SEQ = 8192
DEC_BATCH = 8
DEC_SEQ = 64
PAST_LEN = 1024
CHUNK = 64
RMS_EPS = 1e-6
ROPE_THETA = 10000.0

GLA_HEADS = 8
GLA_DK = 256
GLA_DV = 512
GLA_GATE_RANK = 16
GLA_GATE_NORM = 16.0
QK_A = GLA_HEADS * GLA_DK
V_A = GLA_HEADS * GLA_DV
GLA_MAIN = 2 * QK_A + 2 * V_A

SWA_HEAD_DIM = 64
SWA_HEADS = 64
SWA_KV_HEADS = 8
SWA_GROUP = SWA_HEADS // SWA_KV_HEADS
WINDOW = 128
WIN_CHUNKS = WINDOW // CHUNK
KV_B = SWA_KV_HEADS * SWA_HEAD_DIM

FF_DIM = 14336
N_EXPERTS = 8
TOP_K = 2

T_TOK = SEQ + DEC_BATCH * DEC_SEQ
N_PROMPT_CHUNKS = SEQ // CHUNK
N_CHUNKS = T_TOK // CHUNK
LANE = 128

VMEM_LIMIT = 56 * 1024 * 1024

TM_TOK = 1088
MOE_SUB = 576
MOE_TILE = 4 * MOE_SUB


def _cparams(n_axes):
    return pltpu.CompilerParams(dimension_semantics=("arbitrary",) * n_axes,
                                vmem_limit_bytes=VMEM_LIMIT)


def _top2_route(logits):
    lane = lax.broadcasted_iota(jnp.int32, logits.shape, 1).astype(F32)
    neg = jnp.float32(-jnp.inf)
    lg = jnp.where(lane < N_EXPERTS, logits, neg)
    m1 = jnp.max(lg, axis=-1, keepdims=True)
    i1 = jnp.min(jnp.where(lg == m1, lane, float(LANE)), axis=-1, keepdims=True)
    lg2 = jnp.where(lane == i1, neg, lg)
    m2 = jnp.max(lg2, axis=-1, keepdims=True)
    i2 = jnp.min(jnp.where(lg2 == m2, lane, float(LANE)), axis=-1, keepdims=True)
    e2 = jnp.exp(m2 - m1)
    inv = 1.0 / (1.0 + e2)
    return jnp.where(lane == 0, i1,
                     jnp.where(lane == 1, i2, jnp.where(lane == 2, inv, e2 * inv)))


def _rmsnorm_body(*refs, n_out, has_delta, emit_sum, has_router):
    it = iter(refs)
    x_ref = next(it)
    d_ref = next(it) if has_delta else None
    g_ref = next(it)
    r_ref = next(it) if has_router else None
    sum_ref = next(it) if emit_sum else None
    o_refs = [next(it) for _ in range(n_out)]
    rt_ref = next(it) if has_router else None

    x = x_ref[...]
    if has_delta:
        x = x + d_ref[...].astype(F32)
    if emit_sum:
        sum_ref[...] = x
    y = x * lax.rsqrt(jnp.mean(x * x, axis=-1, keepdims=True) + RMS_EPS)
    for i, o_ref in enumerate(o_refs):
        yi = y * g_ref[i:i + 1, :]
        if o_ref.dtype == jnp.uint32:
            o_ref[...] = _pack_halves(yi)
        else:
            o_ref[...] = yi.astype(o_ref.dtype)
        if has_router and i == 0:
            logits = jnp.dot(yi, r_ref[...], precision=lax.Precision.HIGHEST,
                             preferred_element_type=F32)
            rt_ref[...] = _top2_route(logits)


def _rmsnorm(x, gains, *, delta=None, emit_sum=False, router=None, out_dtype=BF16,
             row_off=0, n_rows=None, tr=256):
    n_rows = x.shape[0] if n_rows is None else n_rows
    d = x.shape[1]
    n_out = gains.shape[0]
    off = row_off // tr
    row_spec = pl.BlockSpec((tr, d), lambda i: (i + off, 0))
    out_spec = pl.BlockSpec((tr, d), lambda i: (i, 0))
    in_specs = [row_spec]
    args = [x]
    if delta is not None:
        in_specs.append(row_spec)
        args.append(delta)
    in_specs.append(pl.BlockSpec((n_out, d), lambda i: (0, 0)))
    args.append(gains)
    if router is not None:
        in_specs.append(pl.BlockSpec(router.shape, lambda i: (0, 0)))
        args.append(router)
    out_shape, out_specs = [], []
    if emit_sum:
        out_shape.append(jax.ShapeDtypeStruct((n_rows, d), F32))
        out_specs.append(out_spec)
    d_out = d // 2 if out_dtype == jnp.uint32 else d
    for _ in range(n_out):
        out_shape.append(jax.ShapeDtypeStruct((n_rows, d_out), out_dtype))
        out_specs.append(pl.BlockSpec((tr, d_out), lambda i: (i, 0)))
    if router is not None:
        out_shape.append(jax.ShapeDtypeStruct((n_rows, router.shape[1]), F32))
        out_specs.append(pl.BlockSpec((tr, router.shape[1]), lambda i: (i, 0)))
    body = functools.partial(_rmsnorm_body, n_out=n_out, has_delta=delta is not None,
                             emit_sum=emit_sum, has_router=router is not None)
    return pl.pallas_call(
        body, grid=(n_rows // tr,), in_specs=in_specs, out_specs=out_specs,
        out_shape=out_shape, compiler_params=_cparams(1), name="rmsnorm")(*args)


def _rope_tile(acc, cos_ref, sin_ref):
    cos = cos_ref[...]
    sin = sin_ref[...]
    lane = lax.broadcasted_iota(jnp.int32, (acc.shape[0], LANE), 1)
    first_half = (lane % SWA_HEAD_DIM) < (SWA_HEAD_DIM // 2)
    outs = []
    for j in range(acc.shape[1] // LANE):
        blk = acc[:, j * LANE:(j + 1) * LANE]
        rot = jnp.where(first_half,
                        pltpu.roll(blk, LANE - SWA_HEAD_DIM // 2, 1),
                        pltpu.roll(blk, SWA_HEAD_DIM // 2, 1))
        outs.append(blk * cos + rot * sin)
    return jnp.concatenate(outs, axis=1)


def _mm_body(*refs, has_bias, has_res, has_rope):
    it = iter(refs)
    x_ref = next(it)
    w_ref = next(it)
    b_ref = next(it) if has_bias else None
    cos_ref = next(it) if has_rope else None
    sin_ref = next(it) if has_rope else None
    res_ref = next(it) if has_res else None
    o_ref = next(it)
    wbf_ref = next(it)

    @pl.when(pl.program_id(1) == 0)
    def _():
        wbf_ref[...] = w_ref[...].astype(BF16)

    acc = jnp.dot(x_ref[...], wbf_ref[...], preferred_element_type=F32)
    if has_bias:
        acc = acc + b_ref[...]
    if has_rope:
        acc = _rope_tile(acc, cos_ref, sin_ref)
    if has_res:
        acc = acc + res_ref[...]
    o_ref[...] = acc.astype(o_ref.dtype)


def _mm(x, w, *, n_cols, tn, tm=TM_TOK, col_off=0, bias=None, res=None, rope=None,
        out_dtype=F32):
    m, k = x.shape
    cb = col_off // tn
    in_specs = [pl.BlockSpec((tm, k), lambda n, i: (i, 0)),
                pl.BlockSpec((k, tn), lambda n, i: (0, n + cb))]
    args = [x, w]
    if bias is not None:
        in_specs.append(pl.BlockSpec((1, tn), lambda n, i: (0, n + cb)))
        args.append(bias)
    if rope is not None:
        for t in rope:
            in_specs.append(pl.BlockSpec((tm, LANE), lambda n, i: (i, 0)))
            args.append(t)
    if res is not None:
        in_specs.append(pl.BlockSpec((tm, tn), lambda n, i: (i, n)))
        args.append(res)
    body = functools.partial(_mm_body, has_bias=bias is not None, has_res=res is not None,
                             has_rope=rope is not None)
    return pl.pallas_call(
        body, grid=(n_cols // tn, m // tm), in_specs=in_specs,
        out_specs=pl.BlockSpec((tm, tn), lambda n, i: (i, n)),
        out_shape=jax.ShapeDtypeStruct((m, n_cols), out_dtype),
        scratch_shapes=[pltpu.VMEM((k, tn), BF16)],
        compiler_params=_cparams(2), name="matmul_ws")(*args)


def _swiglu_tile(x, wg, wu):
    g = jnp.dot(x, wg, preferred_element_type=F32)
    u = jnp.dot(x, wu, preferred_element_type=F32)
    return g * jax.nn.sigmoid(g) * u


def _gateup_body(x_ref, wg_ref, wu_ref, o_ref, wgbf_ref, wubf_ref):
    @pl.when(pl.program_id(1) == 0)
    def _():
        wgbf_ref[...] = wg_ref[...].astype(BF16)
        wubf_ref[...] = wu_ref[...].astype(BF16)

    o_ref[...] = _swiglu_tile(x_ref[...], wgbf_ref[...], wubf_ref[...]).astype(o_ref.dtype)


def _gateup(x, wg, wu, *, tm=TM_TOK, tn=256):
    m, k = x.shape
    f = wg.shape[1]
    return pl.pallas_call(
        _gateup_body, grid=(f // tn, m // tm),
        in_specs=[pl.BlockSpec((tm, k), lambda n, i: (i, 0)),
                  pl.BlockSpec((k, tn), lambda n, i: (0, n)),
                  pl.BlockSpec((k, tn), lambda n, i: (0, n))],
        out_specs=pl.BlockSpec((tm, tn), lambda n, i: (i, n)),
        out_shape=jax.ShapeDtypeStruct((m, f), BF16),
        scratch_shapes=[pltpu.VMEM((k, tn), BF16), pltpu.VMEM((k, tn), BF16)],
        compiler_params=_cparams(2), name="ffn_gateup")(x, wg, wu)


def _pack_halves(y):
    n = y.shape[1] // 2
    lo = lax.bitcast_convert_type(y[:, :n].astype(BF16).astype(F32), jnp.uint32) >> 16
    hi = lax.bitcast_convert_type(y[:, n:].astype(BF16).astype(F32), jnp.uint32)
    return (hi & jnp.uint32(0xFFFF0000)) | lo


def _unpack_halves(u):
    lo = lax.bitcast_convert_type(u << 16, F32).astype(BF16)
    hi = lax.bitcast_convert_type(u & jnp.uint32(0xFFFF0000), F32).astype(BF16)
    return jnp.concatenate([lo, hi], axis=1)


def _moe_gateup_body(te_ref, src_ref, nsub_ref, nrows_ref, tok_ref, xp_hbm, wg_ref, wu_ref,
                     o_ref, x_ref, stage_ref, wgbf_ref, wubf_ref, sem):
    r = pl.program_id(0)
    n = pl.program_id(1)
    nsub = nsub_ref[r]

    def row_copy(tok, i):
        return pltpu.make_async_copy(xp_hbm.at[pl.ds(tok, 1)], stage_ref.at[pl.ds(i, 1)], sem)

    @pl.when(jnp.logical_and(r == 0, n == 0))
    def _():
        stage_ref[...] = jnp.zeros_like(stage_ref)

    @pl.when(jnp.logical_and(n == 0, nsub > 0))
    def _():
        for j in range(MOE_TILE // MOE_SUB):
            @pl.when(j < nsub)
            def _():
                live = jnp.clip(nrows_ref[r] - j * MOE_SUB, 0, MOE_SUB)

                def start(i, carry):
                    row_copy(tok_ref[0, j * MOE_SUB + i], i).start()
                    return carry

                def wait(i, carry):
                    row_copy(0, i).wait()
                    return carry

                lax.fori_loop(0, live, start, 0)
                lax.fori_loop(0, live, wait, 0)
                x_ref[pl.ds(j * MOE_SUB, MOE_SUB), :] = _unpack_halves(stage_ref[...])

    @pl.when(nsub > 0)
    def _():
        wgbf_ref[...] = wg_ref[...].astype(BF16)
        wubf_ref[...] = wu_ref[...].astype(BF16)
        for j in range(MOE_TILE // MOE_SUB):
            rows = pl.ds(j * MOE_SUB, MOE_SUB)

            @pl.when(j < nsub)
            def _():
                o_ref[rows, :] = _swiglu_tile(x_ref[rows, :], wgbf_ref[...],
                                              wubf_ref[...]).astype(o_ref.dtype)

            @pl.when(j >= nsub)
            def _():
                o_ref[rows, :] = jnp.zeros((MOE_SUB, o_ref.shape[1]), o_ref.dtype)


def _moe_gateup(meta, slot_tok, xp, wg, wu, *, tn=256):
    n_tiles = slot_tok.shape[0]
    kp = xp.shape[1]
    k, f = wg.shape[1], wg.shape[2]
    nn = f // tn

    def col(n, r, ns):
        return jnp.where(ns[r] > 0, n, nn - 1)

    grid_spec = pltpu.PrefetchScalarGridSpec(
        num_scalar_prefetch=4, grid=(n_tiles, nn),
        in_specs=[pl.BlockSpec((None, 1, MOE_TILE), lambda r, n, te, src, ns, nr: (src[r], 0, 0),
                               memory_space=pltpu.SMEM),
                  pl.BlockSpec(memory_space=pl.ANY),
                  pl.BlockSpec((None, k, tn),
                               lambda r, n, te, src, ns, nr: (te[r], 0, col(n, r, ns))),
                  pl.BlockSpec((None, k, tn),
                               lambda r, n, te, src, ns, nr: (te[r], 0, col(n, r, ns)))],
        out_specs=pl.BlockSpec((MOE_TILE, tn),
                               lambda r, n, te, src, ns, nr: (src[r], col(n, r, ns))),
        scratch_shapes=[pltpu.VMEM((MOE_TILE, k), BF16), pltpu.VMEM((MOE_SUB, kp), jnp.uint32),
                        pltpu.VMEM((k, tn), BF16), pltpu.VMEM((k, tn), BF16),
                        pltpu.SemaphoreType.DMA(())])
    return pl.pallas_call(
        _moe_gateup_body, grid_spec=grid_spec,
        out_shape=jax.ShapeDtypeStruct((n_tiles * MOE_TILE, f), BF16),
        compiler_params=_cparams(2), name="moe_gateup")(*meta, slot_tok, xp, wg, wu)


def _down_body(a_ref, w_ref, res_ref, o_ref, acc_ref):
    kk = pl.program_id(2)

    @pl.when(kk == 0)
    def _():
        acc_ref[...] = jnp.zeros_like(acc_ref)

    acc_ref[...] += jnp.dot(a_ref[...], w_ref[...].astype(BF16), preferred_element_type=F32)

    @pl.when(kk == pl.num_programs(2) - 1)
    def _():
        o_ref[...] = (acc_ref[...] + res_ref[...]).astype(o_ref.dtype)


def _down(a, w, res, *, tm=TM_TOK, tn=1024, tk=1024):
    m, k = a.shape
    n = w.shape[1]
    return pl.pallas_call(
        _down_body, grid=(m // tm, n // tn, k // tk),
        in_specs=[pl.BlockSpec((tm, tk), lambda i, j, kk: (i, kk)),
                  pl.BlockSpec((tk, tn), lambda i, j, kk: (kk, j)),
                  pl.BlockSpec((tm, tn), lambda i, j, kk: (i, j))],
        out_specs=pl.BlockSpec((tm, tn), lambda i, j, kk: (i, j)),
        out_shape=jax.ShapeDtypeStruct((m, n), F32),
        scratch_shapes=[pltpu.VMEM((tm, tn), F32)],
        compiler_params=_cparams(3), name="ffn_down")(a, w, res)


def _moe_down_body(te_ref, src_ref, nsub_ref, nrows_ref, a_ref, w_ref, o_ref, acc_ref, wbf_ref):
    r = pl.program_id(0)
    kk = pl.program_id(2)
    nsub = nsub_ref[r]

    @pl.when(nsub > 0)
    def _():
        @pl.when(kk == 0)
        def _():
            acc_ref[...] = jnp.zeros_like(acc_ref)

        wbf_ref[...] = w_ref[...].astype(BF16)
        for j in range(MOE_TILE // MOE_SUB):
            rows = pl.ds(j * MOE_SUB, MOE_SUB)

            @pl.when(j < nsub)
            def _():
                acc_ref[rows, :] += jnp.dot(a_ref[rows, :], wbf_ref[...],
                                            preferred_element_type=F32)

        @pl.when(kk == pl.num_programs(2) - 1)
        def _():
            o_ref[...] = acc_ref[...].astype(o_ref.dtype)


def _moe_down(meta, a, w, *, tn=1024, tk=1024):
    n_slots, k = a.shape
    n = w.shape[2]
    nn, nk = n // tn, k // tk

    def a_map(r, j, kk, te, src, ns, nr):
        return (src[r], jnp.where(ns[r] > 0, kk, nk - 1))

    def w_map(r, j, kk, te, src, ns, nr):
        live = ns[r] > 0
        return (te[r], jnp.where(live, kk, nk - 1), jnp.where(live, j, nn - 1))

    def o_map(r, j, kk, te, src, ns, nr):
        return (src[r], jnp.where(ns[r] > 0, j, nn - 1))

    grid_spec = pltpu.PrefetchScalarGridSpec(
        num_scalar_prefetch=4, grid=(n_slots // MOE_TILE, nn, nk),
        in_specs=[pl.BlockSpec((MOE_TILE, tk), a_map), pl.BlockSpec((None, tk, tn), w_map)],
        out_specs=pl.BlockSpec((MOE_TILE, tn), o_map),
        scratch_shapes=[pltpu.VMEM((MOE_TILE, tn), F32), pltpu.VMEM((tk, tn), BF16)])
    return pl.pallas_call(
        _moe_down_body, grid_spec=grid_spec,
        out_shape=jax.ShapeDtypeStruct((n_slots, n), BF16),
        compiler_params=_cparams(3), name="moe_down")(*meta, a, w)


def _gla_body(q_ref, k_ref, v_ref, g_ref, a_ref, wa_ref, ba_ref, gn_ref, s0_ref,
              o_ref, s_ref):
    c = pl.program_id(0)

    @pl.when(c == 0)
    def _():
        s_ref[...] = jnp.zeros_like(s_ref)

    @pl.when(c >= N_PROMPT_CHUNKS)
    def _():
        s_ref[...] = s0_ref[...]

    xg = jnp.dot(a_ref[...].astype(BF16), wa_ref[...].astype(BF16),
                 preferred_element_type=F32) + ba_ref[...]
    la = (jnp.minimum(xg, 0.0) - jnp.log1p(jnp.exp(-jnp.abs(xg)))) * (1.0 / GLA_GATE_NORM)

    row = lax.broadcasted_iota(jnp.int32, (CHUNK, CHUNK), 0)
    col = lax.broadcasted_iota(jnp.int32, (CHUNK, CHUNK), 1)
    causal = row >= col
    tri = causal.astype(BF16)
    la_hi = la.astype(BF16)
    la_lo = (la - la_hi.astype(F32)).astype(BF16)
    cum = (jnp.dot(tri, la_hi, preferred_element_type=F32)
           + jnp.dot(tri, la_lo, preferred_element_type=F32))
    last = cum[CHUNK - 1:CHUNK, :]

    q = q_ref[...].astype(F32) * (GLA_DK ** -0.5)
    q_in = q * jnp.exp(cum)
    k_out = (k_ref[...].astype(F32) * jnp.exp(last - cum)).astype(BF16)
    q_loc = (q_in * jnp.exp(-last)).astype(BF16)
    q_in = q_in.astype(BF16)
    dec_rows = jnp.broadcast_to(jnp.exp(last), (LANE, QK_A))

    for h in range(GLA_HEADS):
        dk = slice(h * GLA_DK, (h + 1) * GLA_DK)
        dv = slice(h * GLA_DV, (h + 1) * GLA_DV)
        v = v_ref[:, dv]
        g = g_ref[:, dv].astype(F32)
        s_old = s_ref[h]
        attn = lax.dot_general(q_loc[:, dk], k_out[:, dk], (((1,), (1,)), ((), ())),
                               preferred_element_type=F32)
        attn = jnp.where(causal, attn, 0.0)
        o = (jnp.dot(q_in[:, dk], s_old.astype(BF16), preferred_element_type=F32)
             + jnp.dot(attn.astype(BF16), v, preferred_element_type=F32))
        kv = lax.dot_general(k_out[:, dk], v, (((0,), (0,)), ((), ())),
                             preferred_element_type=F32)
        dec_col = jnp.transpose(dec_rows[:, dk])
        dec = jnp.concatenate([dec_col] * (GLA_DV // LANE), axis=1)
        s_ref[h] = dec * s_old + kv
        on = o * lax.rsqrt(jnp.mean(o * o, axis=-1, keepdims=True) + RMS_EPS) * gn_ref[...]
        o_ref[:, dv] = (on * (g * jax.nn.sigmoid(g))).astype(o_ref.dtype)


def _gla(proj, a_low, w_a2p, b_a, g_norm, s0):
    n_seq = 1 + DEC_BATCH
    state_block = (None, GLA_HEADS, GLA_DK, GLA_DV)

    def s0_map(c):
        return (jnp.maximum(c - N_PROMPT_CHUNKS, 0), 0, 0, 0)

    def s_map(c):
        return (jnp.where(c < N_PROMPT_CHUNKS, 0, c - N_PROMPT_CHUNKS + 1), 0, 0, 0)

    return pl.pallas_call(
        _gla_body, grid=(N_CHUNKS,),
        in_specs=[pl.BlockSpec((CHUNK, QK_A), lambda c: (c, 0)),
                  pl.BlockSpec((CHUNK, QK_A), lambda c: (c, 1)),
                  pl.BlockSpec((CHUNK, V_A), lambda c: (c, 1)),
                  pl.BlockSpec((CHUNK, V_A), lambda c: (c, 2)),
                  pl.BlockSpec((CHUNK, LANE), lambda c: (c, 0)),
                  pl.BlockSpec((LANE, QK_A), lambda c: (0, 0)),
                  pl.BlockSpec((1, QK_A), lambda c: (0, 0)),
                  pl.BlockSpec((1, GLA_DV), lambda c: (0, 0)),
                  pl.BlockSpec(state_block, s0_map)],
        out_specs=[pl.BlockSpec((CHUNK, V_A), lambda c: (c, 0)),
                   pl.BlockSpec(state_block, s_map)],
        out_shape=[jax.ShapeDtypeStruct((T_TOK, V_A), BF16),
                   jax.ShapeDtypeStruct((n_seq, GLA_HEADS, GLA_DK, GLA_DV), F32)],
        compiler_params=_cparams(1), name="gla")(
            proj, proj, proj, proj, a_low, w_a2p, b_a, g_norm, s0)


def _swa_body(q_ref, k0_ref, k1_ref, k2_ref, v0_ref, v1_ref, v2_ref, sink_ref, o_ref):
    c = pl.program_id(0)
    kcat = jnp.concatenate([k0_ref[...], k1_ref[...], k2_ref[...]], axis=0)
    vcat = jnp.concatenate([v0_ref[...], v1_ref[...], v2_ref[...]], axis=0)
    n_keys = (WIN_CHUNKS + 1) * CHUNK
    n_pairs = SWA_GROUP // 2
    n_q = n_pairs * CHUNK
    key_chunk = lax.broadcasted_iota(jnp.int32, (n_keys, n_q), 0) // CHUNK
    valid = jnp.logical_or(c >= N_PROMPT_CHUNKS, c - WIN_CHUNKS + key_chunk >= 0)
    low_half = lax.broadcasted_iota(jnp.int32, (n_keys, LANE), 1) < SWA_HEAD_DIM
    scale = SWA_HEAD_DIM ** -0.5

    for slab in range(KV_B // LANE):
        ks = kcat[:, slab * LANE:(slab + 1) * LANE]
        vs = vcat[:, slab * LANE:(slab + 1) * LANE]
        ks_sw = pltpu.roll(ks, SWA_HEAD_DIM, 1)
        vs_sw = pltpu.roll(vs, SWA_HEAD_DIM, 1)
        for par in range(2):
            kh = 2 * slab + par
            k_lo, k_hi = (ks, ks_sw) if par == 0 else (ks_sw, ks)
            v_lo, v_hi = (vs, vs_sw) if par == 0 else (vs_sw, vs)
            k_half = (jnp.where(low_half, k_lo, 0.0).astype(BF16),
                      jnp.where(low_half, 0.0, k_hi).astype(BF16))
            v_half = (jnp.where(low_half, v_lo, 0.0).astype(BF16),
                      jnp.where(low_half, 0.0, v_hi).astype(BF16))
            first = kh * n_pairs
            qs = jnp.concatenate([q_ref[:, (first + p) * LANE:(first + p + 1) * LANE]
                                  for p in range(n_pairs)], axis=0)
            acc = None
            for parity in range(2):
                s = lax.dot_general(k_half[parity], qs, (((1,), (1,)), ((), ())),
                                    preferred_element_type=F32) * scale
                s = jnp.where(valid, s, -1e30)
                sink = sink_ref[kh, parity][0:1, :]
                m = jnp.maximum(jnp.max(s, axis=0, keepdims=True), sink)
                p = jnp.exp(s - m)
                denom = jnp.sum(p, axis=0, keepdims=True) + jnp.exp(sink - m)
                pn = (p * (1.0 / denom)).astype(BF16)
                o = lax.dot_general(pn, v_half[parity], (((0,), (0,)), ((), ())),
                                    preferred_element_type=F32)
                acc = o if acc is None else acc + o
            for p in range(n_pairs):
                o_ref[:, (first + p) * LANE:(first + p + 1) * LANE] = (
                    acc[p * CHUNK:(p + 1) * CHUNK, :].astype(o_ref.dtype))


def _swa(q, kf, vf, sink_rows):
    def kv_map(j):
        def index(c):
            prompt = jnp.maximum(c - WIN_CHUNKS + j, 0)
            sample = N_PROMPT_CHUNKS + (WIN_CHUNKS + 1) * (c - N_PROMPT_CHUNKS) + j
            return (jnp.where(c < N_PROMPT_CHUNKS, prompt, sample), 0)
        return index

    kv_specs = [pl.BlockSpec((CHUNK, KV_B), kv_map(j)) for j in range(WIN_CHUNKS + 1)]
    return pl.pallas_call(
        _swa_body, grid=(N_CHUNKS,),
        in_specs=[pl.BlockSpec((CHUNK, D_MODEL), lambda c: (c, 0))] + kv_specs + kv_specs
        + [pl.BlockSpec(sink_rows.shape, lambda c: (0, 0, 0, 0))],
        out_specs=pl.BlockSpec((CHUNK, D_MODEL), lambda c: (c, 0)),
        out_shape=jax.ShapeDtypeStruct((T_TOK, D_MODEL), BF16),
        compiler_params=_cparams(1), name="swa")(q, kf, kf, kf, vf, vf, vf, sink_rows)


def _route(route):
    n_tok = route.shape[0]
    top_idx = route[:, 0:TOP_K].astype(jnp.int32)
    gates = route[:, TOP_K:2 * TOP_K]
    flat_e = top_idx.reshape(-1)
    n_assign = n_tok * TOP_K
    onehot = (flat_e[:, None] == jnp.arange(N_EXPERTS, dtype=jnp.int32)[None, :]).astype(jnp.int32)
    csum = jnp.cumsum(onehot, axis=0)
    rank = jnp.take_along_axis(csum, flat_e[:, None], axis=1)[:, 0] - 1
    counts = csum[-1]
    n_tiles_e = (counts + MOE_TILE - 1) // MOE_TILE
    tile_end = jnp.cumsum(n_tiles_e)
    tile_start = tile_end - n_tiles_e
    dest = tile_start[flat_e] * MOE_TILE + rank
    n_tiles = -(-n_assign // MOE_TILE) + N_EXPERTS
    n_slots = n_tiles * MOE_TILE
    slot_tok = jnp.zeros((n_slots,), jnp.int32).at[dest].set(
        jnp.arange(n_assign, dtype=jnp.int32) // TOP_K)
    n_used = tile_end[-1]
    r = jnp.arange(n_tiles, dtype=jnp.int32)
    src = jnp.minimum(r, n_used - 1)
    te = jnp.clip(jnp.searchsorted(tile_end, src, side='right'), 0, N_EXPERTS - 1).astype(jnp.int32)
    nrows = jnp.where(r < n_used,
                      jnp.clip(counts[te] - (r - tile_start[te]) * MOE_TILE, 0, MOE_TILE), 0)
    nsub = (nrows + MOE_SUB - 1) // MOE_SUB
    meta = (te, src.astype(jnp.int32), nsub.astype(jnp.int32), nrows.astype(jnp.int32))
    return meta, slot_tok.reshape(n_tiles, 1, MOE_TILE), dest.reshape(n_tok, TOP_K), gates


def kernel(x_prompt, x_sample, state_gla, cache_swa_k, cache_swa_v, ln_mix, ln_ff, ln_kv, ln_out, gla_w_in, gla_w_a2, gla_b_a, gla_norm, gla_w_o, kv_w, kv_b, swa_w_q, swa_b_q, swa_sinks, swa_w_o, swa_b_o, ffd_w_gate, ffd_w_up, ffd_w_down, moe_router, moe_w_gate, moe_w_up, moe_w_down):
    x = jnp.concatenate([x_prompt.reshape(SEQ, D_MODEL),
                         x_sample.reshape(DEC_BATCH * DEC_SEQ, D_MODEL)], axis=0)

    (hn,) = _rmsnorm(x, ln_mix[0:1])
    w_in = gla_w_in[0]
    proj = _mm(hn, w_in, n_cols=GLA_MAIN, tn=512, out_dtype=BF16)
    w_low = jnp.pad(w_in[:, GLA_MAIN:], ((0, 0), (0, LANE - GLA_GATE_RANK)))
    a_low = _mm(hn, w_low, n_cols=LANE, tn=LANE, out_dtype=F32)
    w_a2p = jnp.pad(gla_w_a2[0], ((0, LANE - GLA_GATE_RANK), (0, 0)))
    o_gla, states = _gla(proj, a_low, w_a2p, gla_b_a[0:1], gla_norm[0:1], state_gla[0])
    h = _mm(o_gla, gla_w_o[0], n_cols=D_MODEL, tn=512, res=x, out_dtype=F32)

    (hn,) = _rmsnorm(h, ln_ff[0:1])
    act = _gateup(hn, ffd_w_gate[0], ffd_w_up[0])
    h = _down(act, ffd_w_down[0], h)

    hkv, hn = _rmsnorm(h, jnp.stack([ln_kv, ln_mix[1]]))
    pos = jnp.concatenate([jnp.arange(SEQ, dtype=jnp.int32),
                           jnp.tile(PAST_LEN + jnp.arange(DEC_SEQ, dtype=jnp.int32), DEC_BATCH)])
    half = SWA_HEAD_DIM // 2
    inv_freq = ROPE_THETA ** (-jnp.arange(half, dtype=F32) / half)
    ang = pos.astype(F32)[:, None] * inv_freq[None, :]
    cos, sin = jnp.cos(ang), jnp.sin(ang)
    rope = (jnp.concatenate([cos, cos, cos, cos], axis=1),
            jnp.concatenate([-sin, sin, -sin, sin], axis=1))
    kv_bias = kv_b[None, :]
    k_new = _mm(hkv, kv_w, n_cols=KV_B, tn=512, bias=kv_bias, rope=rope, out_dtype=F32)
    v_new = _mm(hkv, kv_w, n_cols=KV_B, tn=512, col_off=KV_B, bias=kv_bias, out_dtype=F32)
    q = _mm(hn, swa_w_q[0], n_cols=D_MODEL, tn=512, bias=swa_b_q[0:1], rope=rope,
            out_dtype=BF16)

    cache_rows = cache_swa_k.shape[1]

    def with_cache(new, cache):
        return jnp.concatenate([cache.reshape(DEC_BATCH, cache_rows, KV_B),
                                new[SEQ:].reshape(DEC_BATCH, DEC_SEQ, KV_B)], axis=1)

    k_samp = with_cache(k_new, cache_swa_k)
    v_samp = with_cache(v_new, cache_swa_v)
    kf = jnp.concatenate([k_new[:SEQ], k_samp.reshape(-1, KV_B)], axis=0)
    vf = jnp.concatenate([v_new[:SEQ], v_samp.reshape(-1, KV_B)], axis=0)
    sinks = swa_sinks[0].reshape(SWA_KV_HEADS, SWA_GROUP // 2, 2).transpose(0, 2, 1)
    sink_rows = jnp.broadcast_to(
        jnp.repeat(sinks, CHUNK, axis=2)[:, :, None, :],
        (SWA_KV_HEADS, 2, 8, (SWA_GROUP // 2) * CHUNK))
    o_swa = _swa(q, kf, vf, sink_rows)
    h = _mm(o_swa, swa_w_o[0], n_cols=D_MODEL, tn=512, bias=swa_b_o[0:1], res=h, out_dtype=F32)

    router = jnp.pad(moe_router[0], ((0, 0), (0, LANE - N_EXPERTS)))
    hn_packed, route = _rmsnorm(h, ln_ff[1:2], router=router, out_dtype=jnp.uint32)
    meta, slot_tok, dest, gates = _route(route)
    act = _moe_gateup(meta, slot_tok, hn_packed, moe_w_gate[0], moe_w_up[0])
    yb = _moe_down(meta, act, moe_w_down[0])
    ff = (yb[dest[:, 0]].astype(F32) * gates[:, 0:1]
          + yb[dest[:, 1]].astype(F32) * gates[:, 1:2])

    (y_prompt,) = _rmsnorm(h, ln_out[None, :], delta=ff, out_dtype=F32, n_rows=SEQ)
    (y_sample,) = _rmsnorm(h, ln_out[None, :], delta=ff, out_dtype=F32, row_off=SEQ,
                           n_rows=DEC_BATCH * DEC_SEQ)

    kv_shape = (SWA_KV_HEADS, SWA_HEAD_DIM)
    return (y_prompt.reshape(1, SEQ, D_MODEL),
            y_sample.reshape(DEC_BATCH, DEC_SEQ, D_MODEL),
            states[0:1][None],
            k_new[SEQ - WINDOW:SEQ].reshape(1, WINDOW, *kv_shape),
            v_new[SEQ - WINDOW:SEQ].reshape(1, WINDOW, *kv_shape),
            states[1:][None],
            k_samp[:, -cache_rows:].reshape(DEC_BATCH, cache_rows, *kv_shape),
            v_samp[:, -cache_rows:].reshape(DEC_BATCH, cache_rows, *kv_shape))
```

```python
import functools

import jax
import jax.numpy as jnp
from jax import lax
from jax.experimental import pallas as pl
from jax.experimental.pallas import tpu as pltpu

F32 = jnp.float32
BF16 = jnp.bfloat16

D_MODEL = 4096
SEQ = 8192
DEC_BATCH = 8
DEC_SEQ = 64
PAST_LEN = 1024
CHUNK = 64
RMS_EPS = 1e-6
ROPE_THETA = 10000.0

GLA_HEADS = 8
GLA_DK = 256
GLA_DV = 512
GLA_GATE_RANK = 16
GLA_GATE_NORM = 16.0
QK_A = GLA_HEADS * GLA_DK
V_A = GLA_HEADS * GLA_DV
GLA_MAIN = 2 * QK_A + 2 * V_A

SWA_HEAD_DIM = 64
SWA_HEADS = 64
SWA_KV_HEADS = 8
SWA_GROUP = SWA_HEADS // SWA_KV_HEADS
WINDOW = 128
WIN_CHUNKS = WINDOW // CHUNK
KV_B = SWA_KV_HEADS * SWA_HEAD_DIM

FF_DIM = 14336
N_EXPERTS = 8
TOP_K = 2

T_TOK = SEQ + DEC_BATCH * DEC_SEQ
N_PROMPT_CHUNKS = SEQ // CHUNK
N_CHUNKS = T_TOK // CHUNK
LANE = 128

VMEM_LIMIT = 56 * 1024 * 1024

TM_TOK = 1088
MOE_SUB = 576
MOE_TILE = 4 * MOE_SUB
MOE_PIECE = 64


def _cparams(n_axes):
    return pltpu.CompilerParams(dimension_semantics=("arbitrary",) * n_axes,
                                vmem_limit_bytes=VMEM_LIMIT)


def _top2_route(logits, counts):
    rows = logits.shape[0]
    lane = lax.broadcasted_iota(jnp.int32, logits.shape, 1).astype(F32)
    neg = jnp.float32(-jnp.inf)
    lg = jnp.where(lane < N_EXPERTS, logits, neg)
    m1 = jnp.max(lg, axis=-1, keepdims=True)
    i1 = jnp.min(jnp.where(lg == m1, lane, float(LANE)), axis=-1, keepdims=True)
    lg2 = jnp.where(lane == i1, neg, lg)
    m2 = jnp.max(lg2, axis=-1, keepdims=True)
    i2 = jnp.min(jnp.where(lg2 == m2, lane, float(LANE)), axis=-1, keepdims=True)
    e2 = jnp.exp(m2 - m1)
    inv = 1.0 / (1.0 + e2)
    pick1 = lane == i1
    pick2 = lane == i2
    tri = (lax.broadcasted_iota(jnp.int32, (rows, rows), 0)
           >= lax.broadcasted_iota(jnp.int32, (rows, rows), 1)).astype(BF16)
    pre1 = jnp.dot(tri, pick1.astype(BF16), preferred_element_type=F32)
    pre2 = jnp.dot(tri, pick2.astype(BF16), preferred_element_type=F32)
    after1 = counts + pre1[rows - 1:rows, :]
    rank1 = jnp.sum(jnp.where(pick1, pre1 + counts, 0.0), axis=-1, keepdims=True) - 1.0
    rank2 = jnp.sum(jnp.where(pick2, pre2 + after1, 0.0), axis=-1, keepdims=True) - 1.0
    out = jnp.where(lane == 0, i1, jnp.where(lane == 1, i2, jnp.where(lane == 2, inv, e2 * inv)))
    out = jnp.where(lane == 4, rank1, jnp.where(lane == 5, rank2, out))
    return out, after1 + pre2[rows - 1:rows, :]


def _rmsnorm_body(*refs, n_out, has_delta, emit_sum, has_router):
    it = iter(refs)
    x_ref = next(it)
    d_ref = next(it) if has_delta else None
    g_ref = next(it)
    r_ref = next(it) if has_router else None
    sum_ref = next(it) if emit_sum else None
    o_refs = [next(it) for _ in range(n_out)]
    rt_ref = next(it) if has_router else None
    cnt_ref = next(it) if has_router else None

    if has_router:
        @pl.when(pl.program_id(0) == 0)
        def _():
            cnt_ref[...] = jnp.zeros_like(cnt_ref)

    x = x_ref[...]
    if has_delta:
        x = x + d_ref[...].astype(F32)
    if emit_sum:
        sum_ref[...] = x
    y = x * lax.rsqrt(jnp.mean(x * x, axis=-1, keepdims=True) + RMS_EPS)
    for i, o_ref in enumerate(o_refs):
        yi = y * g_ref[i:i + 1, :]
        if o_ref.dtype == jnp.uint32:
            o_ref[...] = _pack_halves(yi)
        else:
            o_ref[...] = yi.astype(o_ref.dtype)
        if has_router and i == 0:
            logits = jnp.dot(yi, r_ref[...], precision=lax.Precision.HIGHEST,
                             preferred_element_type=F32)
            route, counts = _top2_route(logits, cnt_ref[0:1, :])
            rt_ref[...] = route
            cnt_ref[...] = jnp.broadcast_to(counts, cnt_ref.shape)


def _rmsnorm(x, gains, *, delta=None, emit_sum=False, router=None, out_dtype=BF16,
             row_off=0, n_rows=None, tr=256):
    n_rows = x.shape[0] if n_rows is None else n_rows
    d = x.shape[1]
    n_out = gains.shape[0]
    off = row_off // tr
    row_spec = pl.BlockSpec((tr, d), lambda i: (i + off, 0))
    out_spec = pl.BlockSpec((tr, d), lambda i: (i, 0))
    in_specs = [row_spec]
    args = [x]
    if delta is not None:
        in_specs.append(row_spec)
        args.append(delta)
    in_specs.append(pl.BlockSpec((n_out, d), lambda i: (0, 0)))
    args.append(gains)
    if router is not None:
        in_specs.append(pl.BlockSpec(router.shape, lambda i: (0, 0)))
        args.append(router)
    out_shape, out_specs = [], []
    if emit_sum:
        out_shape.append(jax.ShapeDtypeStruct((n_rows, d), F32))
        out_specs.append(out_spec)
    d_out = d // 2 if out_dtype == jnp.uint32 else d
    for _ in range(n_out):
        out_shape.append(jax.ShapeDtypeStruct((n_rows, d_out), out_dtype))
        out_specs.append(pl.BlockSpec((tr, d_out), lambda i: (i, 0)))
    if router is not None:
        out_shape.append(jax.ShapeDtypeStruct((n_rows, router.shape[1]), F32))
        out_specs.append(pl.BlockSpec((tr, router.shape[1]), lambda i: (i, 0)))
        out_shape.append(jax.ShapeDtypeStruct((8, router.shape[1]), F32))
        out_specs.append(pl.BlockSpec((8, router.shape[1]), lambda i: (0, 0)))
    body = functools.partial(_rmsnorm_body, n_out=n_out, has_delta=delta is not None,
                             emit_sum=emit_sum, has_router=router is not None)
    return pl.pallas_call(
        body, grid=(n_rows // tr,), in_specs=in_specs, out_specs=out_specs,
        out_shape=out_shape, compiler_params=_cparams(1), name="rmsnorm")(*args)


def _embed_norm_body(xp_ref, xs_ref, g_ref, x_ref, o_ref, *, n_prompt_tiles):
    def emit(x):
        x_ref[...] = x
        y = x * lax.rsqrt(jnp.mean(x * x, axis=-1, keepdims=True) + RMS_EPS)
        o_ref[...] = (y * g_ref[...]).astype(o_ref.dtype)

    @pl.when(pl.program_id(0) < n_prompt_tiles)
    def _():
        emit(xp_ref[...])

    @pl.when(pl.program_id(0) >= n_prompt_tiles)
    def _():
        emit(xs_ref[...])


def _embed_norm(x_prompt, x_sample, gain, *, tr=256):
    d = x_prompt.shape[1]
    n_p, n_s = x_prompt.shape[0] // tr, x_sample.shape[0] // tr
    out_spec = pl.BlockSpec((tr, d), lambda i: (i, 0))
    return pl.pallas_call(
        functools.partial(_embed_norm_body, n_prompt_tiles=n_p), grid=(n_p + n_s,),
        in_specs=[pl.BlockSpec((tr, d), lambda i: (jnp.minimum(i, n_p - 1), 0)),
                  pl.BlockSpec((tr, d), lambda i: (jnp.maximum(i - n_p, 0), 0)),
                  pl.BlockSpec((1, d), lambda i: (0, 0))],
        out_specs=[out_spec, out_spec],
        out_shape=[jax.ShapeDtypeStruct(((n_p + n_s) * tr, d), F32),
                   jax.ShapeDtypeStruct(((n_p + n_s) * tr, d), BF16)],
        compiler_params=_cparams(1), name="embed_norm")(x_prompt, x_sample, gain)


def _rope_tile(acc, cos_ref, sin_ref):
    cos = cos_ref[...]
    sin = sin_ref[...]
    lane = lax.broadcasted_iota(jnp.int32, (acc.shape[0], LANE), 1)
    first_half = (lane % SWA_HEAD_DIM) < (SWA_HEAD_DIM // 2)
    outs = []
    for j in range(acc.shape[1] // LANE):
        blk = acc[:, j * LANE:(j + 1) * LANE]
        rot = jnp.where(first_half,
                        pltpu.roll(blk, LANE - SWA_HEAD_DIM // 2, 1),
                        pltpu.roll(blk, SWA_HEAD_DIM // 2, 1))
        outs.append(blk * cos + rot * sin)
    return jnp.concatenate(outs, axis=1)


def _mm_body(*refs, has_bias, has_res, has_rope):
    it = iter(refs)
    x_ref = next(it)
    w_ref = next(it)
    b_ref = next(it) if has_bias else None
    cos_ref = next(it) if has_rope else None
    sin_ref = next(it) if has_rope else None
    res_ref = next(it) if has_res else None
    o_ref = next(it)
    wbf_ref = next(it)

    @pl.when(pl.program_id(1) == 0)
    def _():
        wbf_ref[...] = w_ref[...].astype(BF16)

    acc = jnp.dot(x_ref[...], wbf_ref[...], preferred_element_type=F32)
    if has_bias:
        acc = acc + b_ref[...]
    if has_rope:
        acc = _rope_tile(acc, cos_ref, sin_ref)
    if has_res:
        acc = acc + res_ref[...]
    o_ref[...] = acc.astype(o_ref.dtype)


def _mm(x, w, *, n_cols, tn, tm=TM_TOK, col_off=0, bias=None, res=None, rope=None,
        out_dtype=F32):
    m, k = x.shape
    cb = col_off // tn
    in_specs = [pl.BlockSpec((tm, k), lambda n, i: (i, 0)),
                pl.BlockSpec((k, tn), lambda n, i: (0, n + cb))]
    args = [x, w]
    if bias is not None:
        in_specs.append(pl.BlockSpec((1, tn), lambda n, i: (0, n + cb)))
        args.append(bias)
    if rope is not None:
        for t in rope:
            in_specs.append(pl.BlockSpec((tm, LANE), lambda n, i: (i, 0)))
            args.append(t)
    if res is not None:
        in_specs.append(pl.BlockSpec((tm, tn), lambda n, i: (i, n)))
        args.append(res)
    body = functools.partial(_mm_body, has_bias=bias is not None, has_res=res is not None,
                             has_rope=rope is not None)
    return pl.pallas_call(
        body, grid=(n_cols // tn, m // tm), in_specs=in_specs,
        out_specs=pl.BlockSpec((tm, tn), lambda n, i: (i, n)),
        out_shape=jax.ShapeDtypeStruct((m, n_cols), out_dtype),
        scratch_shapes=[pltpu.VMEM((k, tn), BF16)],
        compiler_params=_cparams(2), name="matmul_ws")(*args)


def _swiglu_tile(x, wg, wu):
    g = jnp.dot(x, wg, preferred_element_type=F32)
    u = jnp.dot(x, wu, preferred_element_type=F32)
    return g * jax.nn.sigmoid(g) * u


def _gateup_body(x_ref, wg_ref, wu_ref, o_ref, wgbf_ref, wubf_ref):
    @pl.when(pl.program_id(1) == 0)
    def _():
        wgbf_ref[...] = wg_ref[...].astype(BF16)
        wubf_ref[...] = wu_ref[...].astype(BF16)

    o_ref[...] = _swiglu_tile(x_ref[...], wgbf_ref[...], wubf_ref[...]).astype(o_ref.dtype)


def _gateup(x, wg, wu, *, tm=TM_TOK, tn=256):
    m, k = x.shape
    f = wg.shape[1]
    return pl.pallas_call(
        _gateup_body, grid=(f // tn, m // tm),
        in_specs=[pl.BlockSpec((tm, k), lambda n, i: (i, 0)),
                  pl.BlockSpec((k, tn), lambda n, i: (0, n)),
                  pl.BlockSpec((k, tn), lambda n, i: (0, n))],
        out_specs=pl.BlockSpec((tm, tn), lambda n, i: (i, n)),
        out_shape=jax.ShapeDtypeStruct((m, f), BF16),
        scratch_shapes=[pltpu.VMEM((k, tn), BF16), pltpu.VMEM((k, tn), BF16)],
        compiler_params=_cparams(2), name="ffn_gateup")(x, wg, wu)


def _for_row_pieces(nrows, fn):
    nfull = nrows // MOE_SUB
    rem = nrows - nfull * MOE_SUB
    for j in range(MOE_TILE // MOE_SUB):
        @pl.when(j < nfull)
        def _():
            fn(pl.ds(j * MOE_SUB, MOE_SUB))
    start = pl.multiple_of(nfull * MOE_SUB, MOE_PIECE)
    for m in range(MOE_PIECE, MOE_SUB + 1, MOE_PIECE):
        @pl.when(jnp.logical_and(rem > m - MOE_PIECE, rem <= m))
        def _():
            fn(pl.ds(start, m))


def _pack_halves(y):
    n = y.shape[1] // 2
    lo = lax.bitcast_convert_type(y[:, :n].astype(BF16).astype(F32), jnp.uint32) >> 16
    hi = lax.bitcast_convert_type(y[:, n:].astype(BF16).astype(F32), jnp.uint32)
    return (hi & jnp.uint32(0xFFFF0000)) | lo


def _unpack_halves(u):
    lo = lax.bitcast_convert_type(u << 16, F32).astype(BF16)
    hi = lax.bitcast_convert_type(u & jnp.uint32(0xFFFF0000), F32).astype(BF16)
    return jnp.concatenate([lo, hi], axis=1)


def _moe_gateup_body(te_ref, src_ref, nsub_ref, nrows_ref, tok_ref, xp_hbm, wg_ref, wu_ref,
                     o_ref, x_ref, stage_ref, wgbf_ref, wubf_ref, sem):
    r = pl.program_id(0)
    n = pl.program_id(1)
    nsub = nsub_ref[r]

    def row_copy(tok, i):
        return pltpu.make_async_copy(xp_hbm.at[pl.ds(tok, 1)], stage_ref.at[pl.ds(i, 1)], sem)

    @pl.when(jnp.logical_and(r == 0, n == 0))
    def _():
        stage_ref[...] = jnp.zeros_like(stage_ref)

    @pl.when(jnp.logical_and(n == 0, nsub > 0))
    def _():
        for j in range(MOE_TILE // MOE_SUB):
            @pl.when(j < nsub)
            def _():
                live = jnp.clip(nrows_ref[r] - j * MOE_SUB, 0, MOE_SUB)

                def start(i, carry):
                    row_copy(tok_ref[0, j * MOE_SUB + i], i).start()
                    return carry

                def wait(i, carry):
                    row_copy(0, i).wait()
                    return carry

                lax.fori_loop(0, live, start, 0)
                lax.fori_loop(0, live, wait, 0)
                x_ref[pl.ds(j * MOE_SUB, MOE_SUB), :] = _unpack_halves(stage_ref[...])

    @pl.when(nsub > 0)
    def _():
        wgbf_ref[...] = wg_ref[...].astype(BF16)
        wubf_ref[...] = wu_ref[...].astype(BF16)

        def piece(rows):
            o_ref[rows, :] = _swiglu_tile(x_ref[rows, :], wgbf_ref[...],
                                          wubf_ref[...]).astype(o_ref.dtype)

        _for_row_pieces(nrows_ref[r], piece)


def _moe_gateup(meta, slot_tok, xp, wg, wu, *, tn=256):
    n_tiles = slot_tok.shape[0]
    kp = xp.shape[1]
    k, f = wg.shape[1], wg.shape[2]
    nn = f // tn

    def col(n, r, ns):
        return jnp.where(ns[r] > 0, n, nn - 1)

    grid_spec = pltpu.PrefetchScalarGridSpec(
        num_scalar_prefetch=4, grid=(n_tiles, nn),
        in_specs=[pl.BlockSpec((None, 1, MOE_TILE), lambda r, n, te, src, ns, nr: (src[r], 0, 0),
                               memory_space=pltpu.SMEM),
                  pl.BlockSpec(memory_space=pl.ANY),
                  pl.BlockSpec((None, k, tn),
                               lambda r, n, te, src, ns, nr: (te[r], 0, col(n, r, ns))),
                  pl.BlockSpec((None, k, tn),
                               lambda r, n, te, src, ns, nr: (te[r], 0, col(n, r, ns)))],
        out_specs=pl.BlockSpec((MOE_TILE, tn),
                               lambda r, n, te, src, ns, nr: (src[r], col(n, r, ns))),
        scratch_shapes=[pltpu.VMEM((MOE_TILE, k), BF16), pltpu.VMEM((MOE_SUB, kp), jnp.uint32),
                        pltpu.VMEM((k, tn), BF16), pltpu.VMEM((k, tn), BF16),
                        pltpu.SemaphoreType.DMA(())])
    return pl.pallas_call(
        _moe_gateup_body, grid_spec=grid_spec,
        out_shape=jax.ShapeDtypeStruct((n_tiles * MOE_TILE, f), BF16),
        compiler_params=_cparams(2), name="moe_gateup")(*meta, slot_tok, xp, wg, wu)


def _down_body(a_ref, w_ref, res_ref, o_ref, acc_ref):
    kk = pl.program_id(2)

    @pl.when(kk == 0)
    def _():
        acc_ref[...] = jnp.zeros_like(acc_ref)

    acc_ref[...] += jnp.dot(a_ref[...], w_ref[...].astype(BF16), preferred_element_type=F32)

    @pl.when(kk == pl.num_programs(2) - 1)
    def _():
        o_ref[...] = (acc_ref[...] + res_ref[...]).astype(o_ref.dtype)


def _down(a, w, res, *, tm=TM_TOK, tn=1024, tk=2048):
    m, k = a.shape
    n = w.shape[1]
    return pl.pallas_call(
        _down_body, grid=(m // tm, n // tn, k // tk),
        in_specs=[pl.BlockSpec((tm, tk), lambda i, j, kk: (i, kk)),
                  pl.BlockSpec((tk, tn), lambda i, j, kk: (kk, j)),
                  pl.BlockSpec((tm, tn), lambda i, j, kk: (i, j))],
        out_specs=pl.BlockSpec((tm, tn), lambda i, j, kk: (i, j)),
        out_shape=jax.ShapeDtypeStruct((m, n), F32),
        scratch_shapes=[pltpu.VMEM((tm, tn), F32)],
        compiler_params=_cparams(3), name="ffn_down")(a, w, res)


def _moe_down_body(te_ref, src_ref, nsub_ref, nrows_ref, a_ref, w_ref, o_ref, acc_ref, wbf_ref):
    r = pl.program_id(0)
    kk = pl.program_id(2)
    nsub = nsub_ref[r]

    @pl.when(nsub > 0)
    def _():
        @pl.when(kk == 0)
        def _():
            acc_ref[...] = jnp.zeros_like(acc_ref)

        wbf_ref[...] = w_ref[...].astype(BF16)

        def piece(rows):
            acc_ref[rows, :] += jnp.dot(a_ref[rows, :], wbf_ref[...],
                                        preferred_element_type=F32)

        _for_row_pieces(nrows_ref[r], piece)

        @pl.when(kk == pl.num_programs(2) - 1)
        def _():
            o_ref[...] = acc_ref[...].astype(o_ref.dtype)


def _moe_down(meta, a, w, *, tn=1024, tk=1024):
    n_slots, k = a.shape
    n = w.shape[2]
    nn, nk = n // tn, k // tk

    def a_map(r, j, kk, te, src, ns, nr):
        return (src[r], jnp.where(ns[r] > 0, kk, nk - 1))

    def w_map(r, j, kk, te, src, ns, nr):
        live = ns[r] > 0
        return (te[r], jnp.where(live, kk, nk - 1), jnp.where(live, j, nn - 1))

    def o_map(r, j, kk, te, src, ns, nr):
        return (src[r], jnp.where(ns[r] > 0, j, nn - 1))

    grid_spec = pltpu.PrefetchScalarGridSpec(
        num_scalar_prefetch=4, grid=(n_slots // MOE_TILE, nn, nk),
        in_specs=[pl.BlockSpec((MOE_TILE, tk), a_map), pl.BlockSpec((None, tk, tn), w_map)],
        out_specs=pl.BlockSpec((MOE_TILE, tn), o_map),
        scratch_shapes=[pltpu.VMEM((MOE_TILE, tn), F32), pltpu.VMEM((tk, tn), BF16)])
    return pl.pallas_call(
        _moe_down_body, grid_spec=grid_spec,
        out_shape=jax.ShapeDtypeStruct((n_slots, n), BF16),
        compiler_params=_cparams(3), name="moe_down")(*meta, a, w)


def _gla_body(q_ref, k_ref, v_ref, g_ref, a_ref, wa_ref, ba_ref, gn_ref, s0_ref,
              o_ref, s_ref):
    c = pl.program_id(0)

    @pl.when(c == 0)
    def _():
        s_ref[...] = jnp.zeros_like(s_ref)

    @pl.when(c >= N_PROMPT_CHUNKS)
    def _():
        s_ref[...] = s0_ref[...]

    xg = jnp.dot(a_ref[...].astype(BF16), wa_ref[...].astype(BF16),
                 preferred_element_type=F32) + ba_ref[...]
    la = (jnp.minimum(xg, 0.0) - jnp.log1p(jnp.exp(-jnp.abs(xg)))) * (1.0 / GLA_GATE_NORM)

    row = lax.broadcasted_iota(jnp.int32, (CHUNK, CHUNK), 0)
    col = lax.broadcasted_iota(jnp.int32, (CHUNK, CHUNK), 1)
    causal = row >= col
    tri = causal.astype(BF16)
    la_hi = la.astype(BF16)
    la_lo = (la - la_hi.astype(F32)).astype(BF16)
    cum = (jnp.dot(tri, la_hi, preferred_element_type=F32)
           + jnp.dot(tri, la_lo, preferred_element_type=F32))
    last = cum[CHUNK - 1:CHUNK, :]

    q = q_ref[...].astype(F32) * (GLA_DK ** -0.5)
    q_in = q * jnp.exp(cum)
    k_out = (k_ref[...].astype(F32) * jnp.exp(last - cum)).astype(BF16)
    q_loc = (q_in * jnp.exp(-last)).astype(BF16)
    q_in = q_in.astype(BF16)
    dec_rows = jnp.broadcast_to(jnp.exp(last), (LANE, QK_A))

    for h in range(GLA_HEADS):
        dk = slice(h * GLA_DK, (h + 1) * GLA_DK)
        dv = slice(h * GLA_DV, (h + 1) * GLA_DV)
        v = v_ref[:, dv]
        g = g_ref[:, dv].astype(F32)
        s_old = s_ref[h]
        attn = lax.dot_general(q_loc[:, dk], k_out[:, dk], (((1,), (1,)), ((), ())),
                               preferred_element_type=F32)
        attn = jnp.where(causal, attn, 0.0)
        o = (jnp.dot(q_in[:, dk], s_old.astype(BF16), preferred_element_type=F32)
             + jnp.dot(attn.astype(BF16), v, preferred_element_type=F32))
        kv = lax.dot_general(k_out[:, dk], v, (((0,), (0,)), ((), ())),
                             preferred_element_type=F32)
        dec_col = jnp.transpose(dec_rows[:, dk])
        dec = jnp.concatenate([dec_col] * (GLA_DV // LANE), axis=1)
        s_ref[h] = dec * s_old + kv
        on = o * lax.rsqrt(jnp.mean(o * o, axis=-1, keepdims=True) + RMS_EPS) * gn_ref[...]
        o_ref[:, dv] = (on * (g * jax.nn.sigmoid(g))).astype(o_ref.dtype)


def _gla(proj, a_low, w_a2p, b_a, g_norm, s0):
    n_seq = 1 + DEC_BATCH
    state_block = (None, GLA_HEADS, GLA_DK, GLA_DV)

    def s0_map(c):
        return (jnp.maximum(c - N_PROMPT_CHUNKS, 0), 0, 0, 0)

    def s_map(c):
        return (jnp.where(c < N_PROMPT_CHUNKS, 0, c - N_PROMPT_CHUNKS + 1), 0, 0, 0)

    return pl.pallas_call(
        _gla_body, grid=(N_CHUNKS,),
        in_specs=[pl.BlockSpec((CHUNK, QK_A), lambda c: (c, 0)),
                  pl.BlockSpec((CHUNK, QK_A), lambda c: (c, 1)),
                  pl.BlockSpec((CHUNK, V_A), lambda c: (c, 1)),
                  pl.BlockSpec((CHUNK, V_A), lambda c: (c, 2)),
                  pl.BlockSpec((CHUNK, LANE), lambda c: (c, 0)),
                  pl.BlockSpec((LANE, QK_A), lambda c: (0, 0)),
                  pl.BlockSpec((1, QK_A), lambda c: (0, 0)),
                  pl.BlockSpec((1, GLA_DV), lambda c: (0, 0)),
                  pl.BlockSpec(state_block, s0_map)],
        out_specs=[pl.BlockSpec((CHUNK, V_A), lambda c: (c, 0)),
                   pl.BlockSpec(state_block, s_map)],
        out_shape=[jax.ShapeDtypeStruct((T_TOK, V_A), BF16),
                   jax.ShapeDtypeStruct((n_seq, GLA_HEADS, GLA_DK, GLA_DV), F32)],
        compiler_params=_cparams(1), name="gla")(
            proj, proj, proj, proj, a_low, w_a2p, b_a, g_norm, s0)


def _swa_body(q_ref, k0_ref, k1_ref, k2_ref, v0_ref, v1_ref, v2_ref, sink_ref, o_ref):
    c = pl.program_id(0)
    kcat = jnp.concatenate([k0_ref[...], k1_ref[...], k2_ref[...]], axis=0)
    vcat = jnp.concatenate([v0_ref[...], v1_ref[...], v2_ref[...]], axis=0)
    n_keys = (WIN_CHUNKS + 1) * CHUNK
    n_pairs = SWA_GROUP // 2
    n_q = n_pairs * CHUNK
    key_chunk = lax.broadcasted_iota(jnp.int32, (n_keys, n_q), 0) // CHUNK
    valid = jnp.logical_or(c >= N_PROMPT_CHUNKS, c - WIN_CHUNKS + key_chunk >= 0)
    low_half = lax.broadcasted_iota(jnp.int32, (n_keys, LANE), 1) < SWA_HEAD_DIM
    scale = SWA_HEAD_DIM ** -0.5

    for slab in range(KV_B // LANE):
        ks = kcat[:, slab * LANE:(slab + 1) * LANE]
        vs = vcat[:, slab * LANE:(slab + 1) * LANE]
        ks_sw = pltpu.roll(ks, SWA_HEAD_DIM, 1)
        vs_sw = pltpu.roll(vs, SWA_HEAD_DIM, 1)
        for par in range(2):
            kh = 2 * slab + par
            k_lo, k_hi = (ks, ks_sw) if par == 0 else (ks_sw, ks)
            v_lo, v_hi = (vs, vs_sw) if par == 0 else (vs_sw, vs)
            k_half = (jnp.where(low_half, k_lo, 0.0).astype(BF16),
                      jnp.where(low_half, 0.0, k_hi).astype(BF16))
            v_half = (jnp.where(low_half, v_lo, 0.0).astype(BF16),
                      jnp.where(low_half, 0.0, v_hi).astype(BF16))
            first = kh * n_pairs
            qs = jnp.concatenate([q_ref[:, (first + p) * LANE:(first + p + 1) * LANE]
                                  for p in range(n_pairs)], axis=0)
            acc = None
            for parity in range(2):
                s = lax.dot_general(k_half[parity], qs, (((1,), (1,)), ((), ())),
                                    preferred_element_type=F32) * scale
                s = jnp.where(valid, s, -1e30)
                sink = sink_ref[kh, parity][0:1, :]
                m = jnp.maximum(jnp.max(s, axis=0, keepdims=True), sink)
                p = jnp.exp(s - m)
                denom = jnp.sum(p, axis=0, keepdims=True) + jnp.exp(sink - m)
                pn = (p * (1.0 / denom)).astype(BF16)
                o = lax.dot_general(pn, v_half[parity], (((0,), (0,)), ((), ())),
                                    preferred_element_type=F32)
                acc = o if acc is None else acc + o
            for p in range(n_pairs):
                o_ref[:, (first + p) * LANE:(first + p + 1) * LANE] = (
                    acc[p * CHUNK:(p + 1) * CHUNK, :].astype(o_ref.dtype))


def _swa(q, kf, vf, sink_rows):
    def kv_map(j):
        def index(c):
            prompt = jnp.maximum(c - WIN_CHUNKS + j, 0)
            sample = N_PROMPT_CHUNKS + (WIN_CHUNKS + 1) * (c - N_PROMPT_CHUNKS) + j
            return (jnp.where(c < N_PROMPT_CHUNKS, prompt, sample), 0)
        return index

    kv_specs = [pl.BlockSpec((CHUNK, KV_B), kv_map(j)) for j in range(WIN_CHUNKS + 1)]
    return pl.pallas_call(
        _swa_body, grid=(N_CHUNKS,),
        in_specs=[pl.BlockSpec((CHUNK, D_MODEL), lambda c: (c, 0))] + kv_specs + kv_specs
        + [pl.BlockSpec(sink_rows.shape, lambda c: (0, 0, 0, 0))],
        out_specs=pl.BlockSpec((CHUNK, D_MODEL), lambda c: (c, 0)),
        out_shape=jax.ShapeDtypeStruct((T_TOK, D_MODEL), BF16),
        compiler_params=_cparams(1), name="swa")(q, kf, kf, kf, vf, vf, vf, sink_rows)


def _route(route, counts):
    n_tok = route.shape[0]
    flat_e = route[:, 0:TOP_K].astype(jnp.int32).reshape(-1)
    gates = route[:, TOP_K:2 * TOP_K]
    rank = route[:, 2 * TOP_K:3 * TOP_K].astype(jnp.int32).reshape(-1)
    counts = counts[0, :N_EXPERTS].astype(jnp.int32)
    n_assign = n_tok * TOP_K
    n_tiles_e = (counts + MOE_TILE - 1) // MOE_TILE
    tile_end = jnp.cumsum(n_tiles_e)
    tile_start = tile_end - n_tiles_e
    dest = tile_start[flat_e] * MOE_TILE + rank
    n_tiles = -(-n_assign // MOE_TILE) + N_EXPERTS
    n_slots = n_tiles * MOE_TILE
    slot_tok = jnp.zeros((n_slots,), jnp.int32).at[dest].set(
        jnp.arange(n_assign, dtype=jnp.int32) // TOP_K)
    n_used = tile_end[-1]
    r = jnp.arange(n_tiles, dtype=jnp.int32)
    src = jnp.minimum(r, n_used - 1)
    te = jnp.clip(jnp.searchsorted(tile_end, src, side='right'), 0, N_EXPERTS - 1).astype(jnp.int32)
    nrows = jnp.where(r < n_used,
                      jnp.clip(counts[te] - (r - tile_start[te]) * MOE_TILE, 0, MOE_TILE), 0)
    nsub = (nrows + MOE_SUB - 1) // MOE_SUB
    meta = (te, src.astype(jnp.int32), nsub.astype(jnp.int32), nrows.astype(jnp.int32))
    return meta, slot_tok.reshape(n_tiles, 1, MOE_TILE), dest.reshape(n_tok, TOP_K), gates


def kernel(x_prompt, x_sample, state_gla, cache_swa_k, cache_swa_v, ln_mix, ln_ff, ln_kv, ln_out, gla_w_in, gla_w_a2, gla_b_a, gla_norm, gla_w_o, kv_w, kv_b, swa_w_q, swa_b_q, swa_sinks, swa_w_o, swa_b_o, ffd_w_gate, ffd_w_up, ffd_w_down, moe_router, moe_w_gate, moe_w_up, moe_w_down):
    x, hn = _embed_norm(x_prompt.reshape(SEQ, D_MODEL),
                        x_sample.reshape(DEC_BATCH * DEC_SEQ, D_MODEL), ln_mix[0:1])
    w_in = gla_w_in[0]
    proj = _mm(hn, w_in, n_cols=GLA_MAIN, tn=512, out_dtype=BF16)
    w_low = jnp.pad(w_in[:, GLA_MAIN:], ((0, 0), (0, LANE - GLA_GATE_RANK)))
    a_low = _mm(hn, w_low, n_cols=LANE, tn=LANE, out_dtype=F32)
    w_a2p = jnp.pad(gla_w_a2[0], ((0, LANE - GLA_GATE_RANK), (0, 0)))
    o_gla, states = _gla(proj, a_low, w_a2p, gla_b_a[0:1], gla_norm[0:1], state_gla[0])
    h = _mm(o_gla, gla_w_o[0], n_cols=D_MODEL, tn=512, res=x, out_dtype=F32)

    (hn,) = _rmsnorm(h, ln_ff[0:1])
    act = _gateup(hn, ffd_w_gate[0], ffd_w_up[0])
    h = _down(act, ffd_w_down[0], h)

    hkv, hn = _rmsnorm(h, jnp.stack([ln_kv, ln_mix[1]]))
    pos = jnp.concatenate([jnp.arange(SEQ, dtype=jnp.int32),
                           jnp.tile(PAST_LEN + jnp.arange(DEC_SEQ, dtype=jnp.int32), DEC_BATCH)])
    half = SWA_HEAD_DIM // 2
    inv_freq = ROPE_THETA ** (-jnp.arange(half, dtype=F32) / half)
    ang = pos.astype(F32)[:, None] * inv_freq[None, :]
    cos, sin = jnp.cos(ang), jnp.sin(ang)
    rope = (jnp.concatenate([cos, cos, cos, cos], axis=1),
            jnp.concatenate([-sin, sin, -sin, sin], axis=1))
    kv_bias = kv_b[None, :]
    k_new = _mm(hkv, kv_w, n_cols=KV_B, tn=512, bias=kv_bias, rope=rope, out_dtype=F32)
    v_new = _mm(hkv, kv_w, n_cols=KV_B, tn=512, col_off=KV_B, bias=kv_bias, out_dtype=F32)
    q = _mm(hn, swa_w_q[0], n_cols=D_MODEL, tn=512, bias=swa_b_q[0:1], rope=rope,
            out_dtype=BF16)

    cache_rows = cache_swa_k.shape[1]

    def with_cache(new, cache):
        return jnp.concatenate([cache.reshape(DEC_BATCH, cache_rows, KV_B),
                                new[SEQ:].reshape(DEC_BATCH, DEC_SEQ, KV_B)], axis=1)

    k_samp = with_cache(k_new, cache_swa_k)
    v_samp = with_cache(v_new, cache_swa_v)
    kf = jnp.concatenate([k_new[:SEQ], k_samp.reshape(-1, KV_B)], axis=0)
    vf = jnp.concatenate([v_new[:SEQ], v_samp.reshape(-1, KV_B)], axis=0)
    sinks = swa_sinks[0].reshape(SWA_KV_HEADS, SWA_GROUP // 2, 2).transpose(0, 2, 1)
    sink_rows = jnp.broadcast_to(
        jnp.repeat(sinks, CHUNK, axis=2)[:, :, None, :],
        (SWA_KV_HEADS, 2, 8, (SWA_GROUP // 2) * CHUNK))
    o_swa = _swa(q, kf, vf, sink_rows)
    h = _mm(o_swa, swa_w_o[0], n_cols=D_MODEL, tn=512, bias=swa_b_o[0:1], res=h, out_dtype=F32)

    router = jnp.pad(moe_router[0], ((0, 0), (0, LANE - N_EXPERTS)))
    hn_packed, route, counts = _rmsnorm(h, ln_ff[1:2], router=router, out_dtype=jnp.uint32)
    meta, slot_tok, dest, gates = _route(route, counts)
    act = _moe_gateup(meta, slot_tok, hn_packed, moe_w_gate[0], moe_w_up[0])
    yb = _moe_down(meta, act, moe_w_down[0])
    ff = (yb[dest[:, 0]].astype(F32) * gates[:, 0:1]
          + yb[dest[:, 1]].astype(F32) * gates[:, 1:2])

    (y_prompt,) = _rmsnorm(h, ln_out[None, :], delta=ff, out_dtype=F32, n_rows=SEQ)
    (y_sample,) = _rmsnorm(h, ln_out[None, :], delta=ff, out_dtype=F32, row_off=SEQ,
                           n_rows=DEC_BATCH * DEC_SEQ)

    kv_shape = (SWA_KV_HEADS, SWA_HEAD_DIM)
    return (y_prompt.reshape(1, SEQ, D_MODEL),
            y_sample.reshape(DEC_BATCH, DEC_SEQ, D_MODEL),
            states[0:1][None],
            k_new[SEQ - WINDOW:SEQ].reshape(1, WINDOW, *kv_shape),
            v_new[SEQ - WINDOW:SEQ].reshape(1, WINDOW, *kv_shape),
            states[1:][None],
            k_samp[:, -cache_rows:].reshape(DEC_BATCH, cache_rows, *kv_shape),
            v_samp[:, -cache_rows:].reshape(DEC_BATCH, cache_rows, *kv_shape))
```

```python
import functools

import jax
import jax.numpy as jnp
from jax import lax
from jax.experimental import pallas as pl
from jax.experimental.pallas import tpu as pltpu

F32 = jnp.float32
BF16 = jnp.bfloat16

D_MODEL = 4096
SEQ = 8192
DEC_BATCH = 8
DEC_SEQ = 64
PAST_LEN = 1024
CHUNK = 64
RMS_EPS = 1e-6
ROPE_THETA = 10000.0

GLA_HEADS = 8
GLA_DK = 256
GLA_DV = 512
GLA_GATE_RANK = 16
GLA_GATE_NORM = 16.0
QK_A = GLA_HEADS * GLA_DK
V_A = GLA_HEADS * GLA_DV
GLA_MAIN = 2 * QK_A + 2 * V_A

SWA_HEAD_DIM = 64
SWA_HEADS = 64
SWA_KV_HEADS = 8
SWA_GROUP = SWA_HEADS // SWA_KV_HEADS
WINDOW = 128
WIN_CHUNKS = WINDOW // CHUNK
KV_B = SWA_KV_HEADS * SWA_HEAD_DIM

FF_DIM = 14336
N_EXPERTS = 8
TOP_K = 2

T_TOK = SEQ + DEC_BATCH * DEC_SEQ
N_PROMPT_CHUNKS = SEQ // CHUNK
N_CHUNKS = T_TOK // CHUNK
LANE = 128

VMEM_LIMIT = 56 * 1024 * 1024

TM_TOK = 1088
MOE_SUB = 576
MOE_TILE = 4 * MOE_SUB
MOE_PIECE = 64


def _cparams(n_axes):
    return pltpu.CompilerParams(dimension_semantics=("arbitrary",) * n_axes,
                                vmem_limit_bytes=VMEM_LIMIT)


def _top2_route(logits, counts):
    rows = logits.shape[0]
    lane = lax.broadcasted_iota(jnp.int32, logits.shape, 1).astype(F32)
    neg = jnp.float32(-jnp.inf)
    lg = jnp.where(lane < N_EXPERTS, logits, neg)
    m1 = jnp.max(lg, axis=-1, keepdims=True)
    i1 = jnp.min(jnp.where(lg == m1, lane, float(LANE)), axis=-1, keepdims=True)
    lg2 = jnp.where(lane == i1, neg, lg)
    m2 = jnp.max(lg2, axis=-1, keepdims=True)
    i2 = jnp.min(jnp.where(lg2 == m2, lane, float(LANE)), axis=-1, keepdims=True)
    e2 = jnp.exp(m2 - m1)
    inv = 1.0 / (1.0 + e2)
    pick1 = lane == i1
    pick2 = lane == i2
    tri = (lax.broadcasted_iota(jnp.int32, (rows, rows), 0)
           >= lax.broadcasted_iota(jnp.int32, (rows, rows), 1)).astype(BF16)
    pre1 = jnp.dot(tri, pick1.astype(BF16), preferred_element_type=F32)
    pre2 = jnp.dot(tri, pick2.astype(BF16), preferred_element_type=F32)
    after1 = counts + pre1[rows - 1:rows, :]
    rank1 = jnp.sum(jnp.where(pick1, pre1 + counts, 0.0), axis=-1, keepdims=True) - 1.0
    rank2 = jnp.sum(jnp.where(pick2, pre2 + after1, 0.0), axis=-1, keepdims=True) - 1.0
    out = jnp.where(lane == 0, i1, jnp.where(lane == 1, i2, jnp.where(lane == 2, inv, e2 * inv)))
    out = jnp.where(lane == 4, rank1, jnp.where(lane == 5, rank2, out))
    return out, after1 + pre2[rows - 1:rows, :]


def _rmsnorm_body(*refs, n_out, has_delta, emit_sum, has_router):
    it = iter(refs)
    x_ref = next(it)
    d_ref = next(it) if has_delta else None
    g_ref = next(it)
    r_ref = next(it) if has_router else None
    sum_ref = next(it) if emit_sum else None
    o_refs = [next(it) for _ in range(n_out)]
    rt_ref = next(it) if has_router else None
    cnt_ref = next(it) if has_router else None

    if has_router:
        @pl.when(pl.program_id(0) == 0)
        def _():
            cnt_ref[...] = jnp.zeros_like(cnt_ref)

    x = x_ref[...]
    if has_delta:
        x = x + d_ref[...].astype(F32)
    if emit_sum:
        sum_ref[...] = x
    y = x * lax.rsqrt(jnp.mean(x * x, axis=-1, keepdims=True) + RMS_EPS)
    for i, o_ref in enumerate(o_refs):
        yi = y * g_ref[i:i + 1, :]
        if o_ref.dtype == jnp.uint32:
            o_ref[...] = _pack_halves(yi)
        else:
            o_ref[...] = yi.astype(o_ref.dtype)
        if has_router and i == 0:
            logits = jnp.dot(yi, r_ref[...], precision=lax.Precision.HIGHEST,
                             preferred_element_type=F32)
            route, counts = _top2_route(logits, cnt_ref[0:1, :])
            rt_ref[...] = route
            cnt_ref[...] = jnp.broadcast_to(counts, cnt_ref.shape)


def _rmsnorm(x, gains, *, delta=None, emit_sum=False, router=None, out_dtype=BF16,
             row_off=0, n_rows=None, tr=256):
    n_rows = x.shape[0] if n_rows is None else n_rows
    d = x.shape[1]
    n_out = gains.shape[0]
    off = row_off // tr
    row_spec = pl.BlockSpec((tr, d), lambda i: (i + off, 0))
    out_spec = pl.BlockSpec((tr, d), lambda i: (i, 0))
    in_specs = [row_spec]
    args = [x]
    if delta is not None:
        in_specs.append(row_spec)
        args.append(delta)
    in_specs.append(pl.BlockSpec((n_out, d), lambda i: (0, 0)))
    args.append(gains)
    if router is not None:
        in_specs.append(pl.BlockSpec(router.shape, lambda i: (0, 0)))
        args.append(router)
    out_shape, out_specs = [], []
    if emit_sum:
        out_shape.append(jax.ShapeDtypeStruct((n_rows, d), F32))
        out_specs.append(out_spec)
    d_out = d // 2 if out_dtype == jnp.uint32 else d
    for _ in range(n_out):
        out_shape.append(jax.ShapeDtypeStruct((n_rows, d_out), out_dtype))
        out_specs.append(pl.BlockSpec((tr, d_out), lambda i: (i, 0)))
    if router is not None:
        out_shape.append(jax.ShapeDtypeStruct((n_rows, router.shape[1]), F32))
        out_specs.append(pl.BlockSpec((tr, router.shape[1]), lambda i: (i, 0)))
        out_shape.append(jax.ShapeDtypeStruct((8, router.shape[1]), F32))
        out_specs.append(pl.BlockSpec((8, router.shape[1]), lambda i: (0, 0)))
    body = functools.partial(_rmsnorm_body, n_out=n_out, has_delta=delta is not None,
                             emit_sum=emit_sum, has_router=router is not None)
    return pl.pallas_call(
        body, grid=(n_rows // tr,), in_specs=in_specs, out_specs=out_specs,
        out_shape=out_shape, compiler_params=_cparams(1), name="rmsnorm")(*args)


def _embed_norm_body(xp_ref, xs_ref, g_ref, x_ref, o_ref, *, n_prompt_tiles):
    def emit(x):
        x_ref[...] = x
        y = x * lax.rsqrt(jnp.mean(x * x, axis=-1, keepdims=True) + RMS_EPS)
        o_ref[...] = (y * g_ref[...]).astype(o_ref.dtype)

    @pl.when(pl.program_id(0) < n_prompt_tiles)
    def _():
        emit(xp_ref[...])

    @pl.when(pl.program_id(0) >= n_prompt_tiles)
    def _():
        emit(xs_ref[...])


def _embed_norm(x_prompt, x_sample, gain, *, tr=256):
    d = x_prompt.shape[1]
    n_p, n_s = x_prompt.shape[0] // tr, x_sample.shape[0] // tr
    out_spec = pl.BlockSpec((tr, d), lambda i: (i, 0))
    return pl.pallas_call(
        functools.partial(_embed_norm_body, n_prompt_tiles=n_p), grid=(n_p + n_s,),
        in_specs=[pl.BlockSpec((tr, d), lambda i: (jnp.minimum(i, n_p - 1), 0)),
                  pl.BlockSpec((tr, d), lambda i: (jnp.maximum(i - n_p, 0), 0)),
                  pl.BlockSpec((1, d), lambda i: (0, 0))],
        out_specs=[out_spec, out_spec],
        out_shape=[jax.ShapeDtypeStruct(((n_p + n_s) * tr, d), F32),
                   jax.ShapeDtypeStruct(((n_p + n_s) * tr, d), BF16)],
        compiler_params=_cparams(1), name="embed_norm")(x_prompt, x_sample, gain)


def _rope_tile(acc, cos_ref, sin_ref):
    cos = cos_ref[...]
    sin = sin_ref[...]
    lane = lax.broadcasted_iota(jnp.int32, (acc.shape[0], LANE), 1)
    first_half = (lane % SWA_HEAD_DIM) < (SWA_HEAD_DIM // 2)
    outs = []
    for j in range(acc.shape[1] // LANE):
        blk = acc[:, j * LANE:(j + 1) * LANE]
        rot = jnp.where(first_half,
                        pltpu.roll(blk, LANE - SWA_HEAD_DIM // 2, 1),
                        pltpu.roll(blk, SWA_HEAD_DIM // 2, 1))
        outs.append(blk * cos + rot * sin)
    return jnp.concatenate(outs, axis=1)


def _mm_body(*refs, has_bias, has_res, has_rope, w_rows_are_cols):
    it = iter(refs)
    x_ref = next(it)
    w_ref = next(it)
    b_ref = next(it) if has_bias else None
    cos_ref = next(it) if has_rope else None
    sin_ref = next(it) if has_rope else None
    res_ref = next(it) if has_res else None
    o_ref = next(it)
    wbf_ref = next(it)

    @pl.when(pl.program_id(1) == 0)
    def _():
        w = w_ref[...]
        wbf_ref[...] = (jnp.transpose(w) if w_rows_are_cols else w).astype(BF16)

    acc = jnp.dot(x_ref[...], wbf_ref[...], preferred_element_type=F32)
    if has_bias:
        acc = acc + b_ref[...]
    if has_rope:
        acc = _rope_tile(acc, cos_ref, sin_ref)
    if has_res:
        acc = acc + res_ref[...]
    o_ref[...] = acc.astype(o_ref.dtype)


def _mm(x, w, *, n_cols, tn, tm=TM_TOK, col_off=0, bias=None, res=None, rope=None,
        out_dtype=F32, w_rows_are_cols=False):
    m, k = x.shape
    cb = col_off // tn
    if w_rows_are_cols:
        w_spec = pl.BlockSpec((tn, k), lambda n, i: (n + cb, 0))
    else:
        w_spec = pl.BlockSpec((k, tn), lambda n, i: (0, n + cb))
    in_specs = [pl.BlockSpec((tm, k), lambda n, i: (i, 0)), w_spec]
    args = [x, w]
    if bias is not None:
        in_specs.append(pl.BlockSpec((1, tn), lambda n, i: (0, n + cb)))
        args.append(bias)
    if rope is not None:
        for t in rope:
            in_specs.append(pl.BlockSpec((tm, LANE), lambda n, i: (i, 0)))
            args.append(t)
    if res is not None:
        in_specs.append(pl.BlockSpec((tm, tn), lambda n, i: (i, n)))
        args.append(res)
    body = functools.partial(_mm_body, has_bias=bias is not None, has_res=res is not None,
                             has_rope=rope is not None, w_rows_are_cols=w_rows_are_cols)
    return pl.pallas_call(
        body, grid=(n_cols // tn, m // tm), in_specs=in_specs,
        out_specs=pl.BlockSpec((tm, tn), lambda n, i: (i, n)),
        out_shape=jax.ShapeDtypeStruct((m, n_cols), out_dtype),
        scratch_shapes=[pltpu.VMEM((k, tn), BF16)],
        compiler_params=_cparams(2), name="matmul_ws")(*args)


def _swiglu_tile(x, wg, wu):
    g = jnp.dot(x, wg, preferred_element_type=F32)
    u = jnp.dot(x, wu, preferred_element_type=F32)
    return g * jax.nn.sigmoid(g) * u


def _gateup_body(x_ref, wg_ref, wu_ref, o_ref, wgbf_ref, wubf_ref):
    @pl.when(pl.program_id(1) == 0)
    def _():
        wgbf_ref[...] = wg_ref[...].astype(BF16)
        wubf_ref[...] = wu_ref[...].astype(BF16)

    o_ref[...] = _swiglu_tile(x_ref[...], wgbf_ref[...], wubf_ref[...]).astype(o_ref.dtype)


def _gateup(x, wg, wu, *, tm=TM_TOK, tn=256):
    m, k = x.shape
    f = wg.shape[1]
    return pl.pallas_call(
        _gateup_body, grid=(f // tn, m // tm),
        in_specs=[pl.BlockSpec((tm, k), lambda n, i: (i, 0)),
                  pl.BlockSpec((k, tn), lambda n, i: (0, n)),
                  pl.BlockSpec((k, tn), lambda n, i: (0, n))],
        out_specs=pl.BlockSpec((tm, tn), lambda n, i: (i, n)),
        out_shape=jax.ShapeDtypeStruct((m, f), BF16),
        scratch_shapes=[pltpu.VMEM((k, tn), BF16), pltpu.VMEM((k, tn), BF16)],
        compiler_params=_cparams(2), name="ffn_gateup")(x, wg, wu)


def _for_row_pieces(nrows, fn, fn_first):
    nfull = nrows // MOE_SUB
    rem = nrows - nfull * MOE_SUB
    for j in range(MOE_TILE // MOE_SUB):
        @pl.when(j < nfull)
        def _():
            (fn_first if j == 0 else fn)(pl.ds(j * MOE_SUB, MOE_SUB))
    start = pl.multiple_of(nfull * MOE_SUB, MOE_PIECE)
    for m in range(MOE_PIECE, MOE_SUB + 1, MOE_PIECE):
        @pl.when(jnp.logical_and(rem > m - MOE_PIECE, rem <= m))
        def _():
            fn(pl.ds(start, m))


def _pack_halves(y):
    n = y.shape[1] // 2
    lo = lax.bitcast_convert_type(y[:, :n].astype(BF16).astype(F32), jnp.uint32) >> 16
    hi = lax.bitcast_convert_type(y[:, n:].astype(BF16).astype(F32), jnp.uint32)
    return (hi & jnp.uint32(0xFFFF0000)) | lo


def _unpack_halves(u):
    lo = lax.bitcast_convert_type(u << 16, F32).astype(BF16)
    hi = lax.bitcast_convert_type(u & jnp.uint32(0xFFFF0000), F32).astype(BF16)
    return jnp.concatenate([lo, hi], axis=1)


def _moe_gateup_body(te_ref, src_ref, nsub_ref, nrows_ref, tok_ref, xp_hbm, wg_ref, wu_ref,
                     o_ref, x_ref, stage_ref, wgbf_ref, wubf_ref, sem):
    r = pl.program_id(0)
    n = pl.program_id(1)
    nsub = nsub_ref[r]

    def row_copy(tok, i):
        return pltpu.make_async_copy(xp_hbm.at[pl.ds(tok, 1)], stage_ref.at[pl.ds(i, 1)], sem)

    @pl.when(jnp.logical_and(r == 0, n == 0))
    def _():
        stage_ref[...] = jnp.zeros_like(stage_ref)

    @pl.when(jnp.logical_and(n == 0, nsub > 0))
    def _():
        for j in range(MOE_TILE // MOE_SUB):
            @pl.when(j < nsub)
            def _():
                live = jnp.clip(nrows_ref[r] - j * MOE_SUB, 0, MOE_SUB)

                def start(i, carry):
                    row_copy(tok_ref[0, j * MOE_SUB + i], i).start()
                    return carry

                def wait(i, carry):
                    row_copy(0, i).wait()
                    return carry

                lax.fori_loop(0, live, start, 0)
                lax.fori_loop(0, live, wait, 0)
                x_ref[pl.ds(j * MOE_SUB, MOE_SUB), :] = _unpack_halves(stage_ref[...])

    @pl.when(nsub > 0)
    def _():
        nrows = nrows_ref[r]

        @pl.when(nrows < MOE_SUB)
        def _():
            wgbf_ref[...] = wg_ref[...].astype(BF16)
            wubf_ref[...] = wu_ref[...].astype(BF16)

        def first_piece(rows):
            wg = wg_ref[...].astype(BF16)
            wu = wu_ref[...].astype(BF16)
            wgbf_ref[...] = wg
            wubf_ref[...] = wu
            o_ref[rows, :] = _swiglu_tile(x_ref[rows, :], wg, wu).astype(o_ref.dtype)

        def piece(rows):
            o_ref[rows, :] = _swiglu_tile(x_ref[rows, :], wgbf_ref[...],
                                          wubf_ref[...]).astype(o_ref.dtype)

        _for_row_pieces(nrows, piece, first_piece)


def _moe_gateup(meta, slot_tok, xp, wg, wu, *, tn=256):
    n_tiles = slot_tok.shape[0]
    kp = xp.shape[1]
    k, f = wg.shape[1], wg.shape[2]
    nn = f // tn

    def col(n, r, ns):
        return jnp.where(ns[r] > 0, n, nn - 1)

    grid_spec = pltpu.PrefetchScalarGridSpec(
        num_scalar_prefetch=4, grid=(n_tiles, nn),
        in_specs=[pl.BlockSpec((None, 1, MOE_TILE), lambda r, n, te, src, ns, nr: (src[r], 0, 0),
                               memory_space=pltpu.SMEM),
                  pl.BlockSpec(memory_space=pl.ANY),
                  pl.BlockSpec((None, k, tn),
                               lambda r, n, te, src, ns, nr: (te[r], 0, col(n, r, ns))),
                  pl.BlockSpec((None, k, tn),
                               lambda r, n, te, src, ns, nr: (te[r], 0, col(n, r, ns)))],
        out_specs=pl.BlockSpec((MOE_TILE, tn),
                               lambda r, n, te, src, ns, nr: (src[r], col(n, r, ns))),
        scratch_shapes=[pltpu.VMEM((MOE_TILE, k), BF16), pltpu.VMEM((MOE_SUB, kp), jnp.uint32),
                        pltpu.VMEM((k, tn), BF16), pltpu.VMEM((k, tn), BF16),
                        pltpu.SemaphoreType.DMA(())])
    return pl.pallas_call(
        _moe_gateup_body, grid_spec=grid_spec,
        out_shape=jax.ShapeDtypeStruct((n_tiles * MOE_TILE, f), BF16),
        compiler_params=_cparams(2), name="moe_gateup")(*meta, slot_tok, xp, wg, wu)


def _down_body(a_ref, w_ref, res_ref, o_ref, acc_ref):
    kk = pl.program_id(2)

    @pl.when(kk == 0)
    def _():
        acc_ref[...] = jnp.zeros_like(acc_ref)

    acc_ref[...] += jnp.dot(a_ref[...], w_ref[...].astype(BF16), preferred_element_type=F32)

    @pl.when(kk == pl.num_programs(2) - 1)
    def _():
        o_ref[...] = (acc_ref[...] + res_ref[...]).astype(o_ref.dtype)


def _down(a, w, res, *, tm=TM_TOK, tn=1024, tk=2048):
    m, k = a.shape
    n = w.shape[1]
    return pl.pallas_call(
        _down_body, grid=(m // tm, n // tn, k // tk),
        in_specs=[pl.BlockSpec((tm, tk), lambda i, j, kk: (i, kk)),
                  pl.BlockSpec((tk, tn), lambda i, j, kk: (kk, j)),
                  pl.BlockSpec((tm, tn), lambda i, j, kk: (i, j))],
        out_specs=pl.BlockSpec((tm, tn), lambda i, j, kk: (i, j)),
        out_shape=jax.ShapeDtypeStruct((m, n), F32),
        scratch_shapes=[pltpu.VMEM((tm, tn), F32)],
        compiler_params=_cparams(3), name="ffn_down")(a, w, res)


def _moe_down_body(te_ref, src_ref, nsub_ref, nrows_ref, a_ref, w_ref, o_ref, acc_ref, wbf_ref):
    r = pl.program_id(0)
    kk = pl.program_id(2)
    nsub = nsub_ref[r]

    @pl.when(nsub > 0)
    def _():
        @pl.when(kk == 0)
        def _():
            acc_ref[...] = jnp.zeros_like(acc_ref)

        nrows = nrows_ref[r]

        @pl.when(nrows < MOE_SUB)
        def _():
            wbf_ref[...] = w_ref[...].astype(BF16)

        def first_piece(rows):
            w = w_ref[...].astype(BF16)
            wbf_ref[...] = w
            acc_ref[rows, :] += jnp.dot(a_ref[rows, :], w, preferred_element_type=F32)

        def piece(rows):
            acc_ref[rows, :] += jnp.dot(a_ref[rows, :], wbf_ref[...],
                                        preferred_element_type=F32)

        _for_row_pieces(nrows, piece, first_piece)

        @pl.when(kk == pl.num_programs(2) - 1)
        def _():
            o_ref[...] = acc_ref[...].astype(o_ref.dtype)


def _moe_down(meta, a, w, *, tn=1024, tk=1024):
    n_slots, k = a.shape
    n = w.shape[2]
    nn, nk = n // tn, k // tk

    def a_map(r, j, kk, te, src, ns, nr):
        return (src[r], jnp.where(ns[r] > 0, kk, nk - 1))

    def w_map(r, j, kk, te, src, ns, nr):
        live = ns[r] > 0
        return (te[r], jnp.where(live, kk, nk - 1), jnp.where(live, j, nn - 1))

    def o_map(r, j, kk, te, src, ns, nr):
        return (src[r], jnp.where(ns[r] > 0, j, nn - 1))

    grid_spec = pltpu.PrefetchScalarGridSpec(
        num_scalar_prefetch=4, grid=(n_slots // MOE_TILE, nn, nk),
        in_specs=[pl.BlockSpec((MOE_TILE, tk), a_map), pl.BlockSpec((None, tk, tn), w_map)],
        out_specs=pl.BlockSpec((MOE_TILE, tn), o_map),
        scratch_shapes=[pltpu.VMEM((MOE_TILE, tn), F32), pltpu.VMEM((tk, tn), BF16)])
    return pl.pallas_call(
        _moe_down_body, grid_spec=grid_spec,
        out_shape=jax.ShapeDtypeStruct((n_slots, n), BF16),
        compiler_params=_cparams(3), name="moe_down")(*meta, a, w)


def _gla_body(q_ref, k_ref, v_ref, g_ref, a_ref, wa_ref, ba_ref, gn_ref, s0_ref,
              o_ref, s_ref):
    c = pl.program_id(0)

    @pl.when(c == 0)
    def _():
        s_ref[...] = jnp.zeros_like(s_ref)

    @pl.when(c >= N_PROMPT_CHUNKS)
    def _():
        s_ref[...] = s0_ref[...]

    xg = jnp.dot(a_ref[...].astype(BF16), wa_ref[...].astype(BF16),
                 preferred_element_type=F32) + ba_ref[...]
    la = (jnp.minimum(xg, 0.0) - jnp.log1p(jnp.exp(-jnp.abs(xg)))) * (1.0 / GLA_GATE_NORM)

    row = lax.broadcasted_iota(jnp.int32, (CHUNK, CHUNK), 0)
    col = lax.broadcasted_iota(jnp.int32, (CHUNK, CHUNK), 1)
    causal = row >= col
    tri = causal.astype(BF16)
    la_hi = la.astype(BF16)
    la_lo = (la - la_hi.astype(F32)).astype(BF16)
    cum = (jnp.dot(tri, la_hi, preferred_element_type=F32)
           + jnp.dot(tri, la_lo, preferred_element_type=F32))
    last = cum[CHUNK - 1:CHUNK, :]

    q = q_ref[...].astype(F32) * (GLA_DK ** -0.5)
    q_in = q * jnp.exp(cum)
    k_out = (k_ref[...].astype(F32) * jnp.exp(last - cum)).astype(BF16)
    q_loc = (q_in * jnp.exp(-last)).astype(BF16)
    q_in = q_in.astype(BF16)
    dec_rows = jnp.broadcast_to(jnp.exp(last), (LANE, QK_A))

    for h in range(GLA_HEADS):
        dk = slice(h * GLA_DK, (h + 1) * GLA_DK)
        dv = slice(h * GLA_DV, (h + 1) * GLA_DV)
        v = v_ref[:, dv]
        g = g_ref[:, dv].astype(F32)
        s_old = s_ref[h]
        attn = lax.dot_general(q_loc[:, dk], k_out[:, dk], (((1,), (1,)), ((), ())),
                               preferred_element_type=F32)
        attn = jnp.where(causal, attn, 0.0)
        o = (jnp.dot(q_in[:, dk], s_old.astype(BF16), preferred_element_type=F32)
             + jnp.dot(attn.astype(BF16), v, preferred_element_type=F32))
        kv = lax.dot_general(k_out[:, dk], v, (((0,), (0,)), ((), ())),
                             preferred_element_type=F32)
        dec_col = jnp.transpose(dec_rows[:, dk])
        dec = jnp.concatenate([dec_col] * (GLA_DV // LANE), axis=1)
        s_ref[h] = dec * s_old + kv
        on = o * lax.rsqrt(jnp.mean(o * o, axis=-1, keepdims=True) + RMS_EPS) * gn_ref[...]
        o_ref[:, dv] = (on * (g * jax.nn.sigmoid(g))).astype(o_ref.dtype)


def _gla(proj, a_low, w_a2p, b_a, g_norm, s0):
    n_seq = 1 + DEC_BATCH
    state_block = (None, GLA_HEADS, GLA_DK, GLA_DV)

    def s0_map(c):
        return (jnp.maximum(c - N_PROMPT_CHUNKS, 0), 0, 0, 0)

    def s_map(c):
        return (jnp.where(c < N_PROMPT_CHUNKS, 0, c - N_PROMPT_CHUNKS + 1), 0, 0, 0)

    return pl.pallas_call(
        _gla_body, grid=(N_CHUNKS,),
        in_specs=[pl.BlockSpec((CHUNK, QK_A), lambda c: (c, 0)),
                  pl.BlockSpec((CHUNK, QK_A), lambda c: (c, 1)),
                  pl.BlockSpec((CHUNK, V_A), lambda c: (c, 1)),
                  pl.BlockSpec((CHUNK, V_A), lambda c: (c, 2)),
                  pl.BlockSpec((CHUNK, LANE), lambda c: (c, 0)),
                  pl.BlockSpec((LANE, QK_A), lambda c: (0, 0)),
                  pl.BlockSpec((1, QK_A), lambda c: (0, 0)),
                  pl.BlockSpec((1, GLA_DV), lambda c: (0, 0)),
                  pl.BlockSpec(state_block, s0_map)],
        out_specs=[pl.BlockSpec((CHUNK, V_A), lambda c: (c, 0)),
                   pl.BlockSpec(state_block, s_map)],
        out_shape=[jax.ShapeDtypeStruct((T_TOK, V_A), BF16),
                   jax.ShapeDtypeStruct((n_seq, GLA_HEADS, GLA_DK, GLA_DV), F32)],
        compiler_params=_cparams(1), name="gla")(
            proj, proj, proj, proj, a_low, w_a2p, b_a, g_norm, s0)


def _swa_body(q_ref, k0_ref, k1_ref, k2_ref, v0_ref, v1_ref, v2_ref, sink_ref, o_ref):
    c = pl.program_id(0)
    kcat = jnp.concatenate([k0_ref[...], k1_ref[...], k2_ref[...]], axis=0)
    vcat = jnp.concatenate([v0_ref[...], v1_ref[...], v2_ref[...]], axis=0)
    n_keys = (WIN_CHUNKS + 1) * CHUNK
    n_pairs = SWA_GROUP // 2
    n_q = n_pairs * CHUNK
    key_chunk = lax.broadcasted_iota(jnp.int32, (n_keys, n_q), 0) // CHUNK
    valid = jnp.logical_or(c >= N_PROMPT_CHUNKS, c - WIN_CHUNKS + key_chunk >= 0)
    low_half = lax.broadcasted_iota(jnp.int32, (n_keys, LANE), 1) < SWA_HEAD_DIM
    scale = SWA_HEAD_DIM ** -0.5

    for slab in range(KV_B // LANE):
        ks = kcat[:, slab * LANE:(slab + 1) * LANE]
        vs = vcat[:, slab * LANE:(slab + 1) * LANE]
        ks_sw = pltpu.roll(ks, SWA_HEAD_DIM, 1)
        vs_sw = pltpu.roll(vs, SWA_HEAD_DIM, 1)
        for par in range(2):
            kh = 2 * slab + par
            k_lo, k_hi = (ks, ks_sw) if par == 0 else (ks_sw, ks)
            v_lo, v_hi = (vs, vs_sw) if par == 0 else (vs_sw, vs)
            k_half = (jnp.where(low_half, k_lo, 0.0).astype(BF16),
                      jnp.where(low_half, 0.0, k_hi).astype(BF16))
            v_half = (jnp.where(low_half, v_lo, 0.0).astype(BF16),
                      jnp.where(low_half, 0.0, v_hi).astype(BF16))
            first = kh * n_pairs
            qs = jnp.concatenate([q_ref[:, (first + p) * LANE:(first + p + 1) * LANE]
                                  for p in range(n_pairs)], axis=0)
            acc = None
            for parity in range(2):
                s = lax.dot_general(k_half[parity], qs, (((1,), (1,)), ((), ())),
                                    preferred_element_type=F32) * scale
                s = jnp.where(valid, s, -1e30)
                sink = sink_ref[kh, parity][0:1, :]
                m = jnp.maximum(jnp.max(s, axis=0, keepdims=True), sink)
                p = jnp.exp(s - m)
                denom = jnp.sum(p, axis=0, keepdims=True) + jnp.exp(sink - m)
                pn = (p * (1.0 / denom)).astype(BF16)
                o = lax.dot_general(pn, v_half[parity], (((0,), (0,)), ((), ())),
                                    preferred_element_type=F32)
                acc = o if acc is None else acc + o
            for p in range(n_pairs):
                o_ref[:, (first + p) * LANE:(first + p + 1) * LANE] = (
                    acc[p * CHUNK:(p + 1) * CHUNK, :].astype(o_ref.dtype))


def _swa(q, kf, vf, sink_rows):
    def kv_map(j):
        def index(c):
            prompt = jnp.maximum(c - WIN_CHUNKS + j, 0)
            sample = N_PROMPT_CHUNKS + (WIN_CHUNKS + 1) * (c - N_PROMPT_CHUNKS) + j
            return (jnp.where(c < N_PROMPT_CHUNKS, prompt, sample), 0)
        return index

    kv_specs = [pl.BlockSpec((CHUNK, KV_B), kv_map(j)) for j in range(WIN_CHUNKS + 1)]
    return pl.pallas_call(
        _swa_body, grid=(N_CHUNKS,),
        in_specs=[pl.BlockSpec((CHUNK, D_MODEL), lambda c: (c, 0))] + kv_specs + kv_specs
        + [pl.BlockSpec(sink_rows.shape, lambda c: (0, 0, 0, 0))],
        out_specs=pl.BlockSpec((CHUNK, D_MODEL), lambda c: (c, 0)),
        out_shape=jax.ShapeDtypeStruct((T_TOK, D_MODEL), BF16),
        compiler_params=_cparams(1), name="swa")(q, kf, kf, kf, vf, vf, vf, sink_rows)


def _route(route, counts):
    n_tok = route.shape[0]
    flat_e = route[:, 0:TOP_K].astype(jnp.int32).reshape(-1)
    gates = route[:, TOP_K:2 * TOP_K]
    rank = route[:, 2 * TOP_K:3 * TOP_K].astype(jnp.int32).reshape(-1)
    counts = counts[0, :N_EXPERTS].astype(jnp.int32)
    n_assign = n_tok * TOP_K
    n_tiles_e = (counts + MOE_TILE - 1) // MOE_TILE
    tile_end = jnp.cumsum(n_tiles_e)
    tile_start = tile_end - n_tiles_e
    dest = tile_start[flat_e] * MOE_TILE + rank
    n_tiles = -(-n_assign // MOE_TILE) + N_EXPERTS
    n_slots = n_tiles * MOE_TILE
    slot_tok = jnp.zeros((n_slots,), jnp.int32).at[dest].set(
        jnp.arange(n_assign, dtype=jnp.int32) // TOP_K)
    n_used = tile_end[-1]
    r = jnp.arange(n_tiles, dtype=jnp.int32)
    src = jnp.maximum(jnp.minimum(r, n_used - 1), 0)
    te = jnp.clip(jnp.searchsorted(tile_end, src, side='right'), 0, N_EXPERTS - 1).astype(jnp.int32)
    nrows = jnp.where(r < n_used,
                      jnp.clip(counts[te] - (r - tile_start[te]) * MOE_TILE, 0, MOE_TILE), 0)
    nsub = (nrows + MOE_SUB - 1) // MOE_SUB
    meta = (te, src.astype(jnp.int32), nsub.astype(jnp.int32), nrows.astype(jnp.int32))
    return meta, slot_tok.reshape(n_tiles, 1, MOE_TILE), dest.reshape(n_tok, TOP_K), gates


def kernel(x_prompt, x_sample, state_gla, cache_swa_k, cache_swa_v, ln_mix, ln_ff, ln_kv, ln_out, gla_w_in, gla_w_a2, gla_b_a, gla_norm, gla_w_o, kv_w, kv_b, swa_w_q, swa_b_q, swa_sinks, swa_w_o, swa_b_o, ffd_w_gate, ffd_w_up, ffd_w_down, moe_router, moe_w_gate, moe_w_up, moe_w_down):
    x, hn = _embed_norm(x_prompt.reshape(SEQ, D_MODEL),
                        x_sample.reshape(DEC_BATCH * DEC_SEQ, D_MODEL), ln_mix[0:1])
    w_in_t = jnp.swapaxes(gla_w_in[0], 0, 1)
    proj = _mm(hn, w_in_t, n_cols=GLA_MAIN, tn=512, out_dtype=BF16, w_rows_are_cols=True)
    w_low_t = jnp.pad(w_in_t[GLA_MAIN:, :], ((0, LANE - GLA_GATE_RANK), (0, 0)))
    a_low = _mm(hn, w_low_t, n_cols=LANE, tn=LANE, out_dtype=F32, w_rows_are_cols=True)
    w_a2p = jnp.pad(gla_w_a2[0], ((0, LANE - GLA_GATE_RANK), (0, 0)))
    o_gla, states = _gla(proj, a_low, w_a2p, gla_b_a[0:1], gla_norm[0:1], state_gla[0])
    h = _mm(o_gla, gla_w_o[0], n_cols=D_MODEL, tn=512, res=x, out_dtype=F32)

    (hn,) = _rmsnorm(h, ln_ff[0:1])
    act = _gateup(hn, ffd_w_gate[0], ffd_w_up[0])
    h = _down(act, ffd_w_down[0], h)

    hkv, hn = _rmsnorm(h, jnp.stack([ln_kv, ln_mix[1]]))
    pos = jnp.concatenate([jnp.arange(SEQ, dtype=jnp.int32),
                           jnp.tile(PAST_LEN + jnp.arange(DEC_SEQ, dtype=jnp.int32), DEC_BATCH)])
    half = SWA_HEAD_DIM // 2
    inv_freq = ROPE_THETA ** (-jnp.arange(half, dtype=F32) / half)
    ang = pos.astype(F32)[:, None] * inv_freq[None, :]
    cos, sin = jnp.cos(ang), jnp.sin(ang)
    rope = (jnp.concatenate([cos, cos, cos, cos], axis=1),
            jnp.concatenate([-sin, sin, -sin, sin], axis=1))
    kv_bias = kv_b[None, :]
    k_new = _mm(hkv, kv_w, n_cols=KV_B, tn=512, bias=kv_bias, rope=rope, out_dtype=F32)
    v_new = _mm(hkv, kv_w, n_cols=KV_B, tn=512, col_off=KV_B, bias=kv_bias, out_dtype=F32)
    q = _mm(hn, swa_w_q[0], n_cols=D_MODEL, tn=512, bias=swa_b_q[0:1], rope=rope,
            out_dtype=BF16)

    cache_rows = cache_swa_k.shape[1]

    def with_cache(new, cache):
        return jnp.concatenate([cache.reshape(DEC_BATCH, cache_rows, KV_B),
                                new[SEQ:].reshape(DEC_BATCH, DEC_SEQ, KV_B)], axis=1)

    k_samp = with_cache(k_new, cache_swa_k)
    v_samp = with_cache(v_new, cache_swa_v)
    kf = jnp.concatenate([k_new[:SEQ], k_samp.reshape(-1, KV_B)], axis=0)
    vf = jnp.concatenate([v_new[:SEQ], v_samp.reshape(-1, KV_B)], axis=0)
    sinks = swa_sinks[0].reshape(SWA_KV_HEADS, SWA_GROUP // 2, 2).transpose(0, 2, 1)
    sink_rows = jnp.broadcast_to(
        jnp.repeat(sinks, CHUNK, axis=2)[:, :, None, :],
        (SWA_KV_HEADS, 2, 8, (SWA_GROUP // 2) * CHUNK))
    o_swa = _swa(q, kf, vf, sink_rows)
    h = _mm(o_swa, swa_w_o[0], n_cols=D_MODEL, tn=512, bias=swa_b_o[0:1], res=h, out_dtype=F32)

    router = jnp.pad(moe_router[0], ((0, 0), (0, LANE - N_EXPERTS)))
    hn_packed, route, counts = _rmsnorm(h, ln_ff[1:2], router=router, out_dtype=jnp.uint32)
    meta, slot_tok, dest, gates = _route(route, counts)
    act = _moe_gateup(meta, slot_tok, hn_packed, moe_w_gate[0], moe_w_up[0])
    yb = _moe_down(meta, act, moe_w_down[0])
    ff = (yb[dest[:, 0]].astype(F32) * gates[:, 0:1]
          + yb[dest[:, 1]].astype(F32) * gates[:, 1:2])

    (y_prompt,) = _rmsnorm(h, ln_out[None, :], delta=ff, out_dtype=F32, n_rows=SEQ)
    (y_sample,) = _rmsnorm(h, ln_out[None, :], delta=ff, out_dtype=F32, row_off=SEQ,
                           n_rows=DEC_BATCH * DEC_SEQ)

    kv_shape = (SWA_KV_HEADS, SWA_HEAD_DIM)
    return (y_prompt.reshape(1, SEQ, D_MODEL),
            y_sample.reshape(DEC_BATCH, DEC_SEQ, D_MODEL),
            states[0:1][None],
            k_new[SEQ - WINDOW:SEQ].reshape(1, WINDOW, *kv_shape),
            v_new[SEQ - WINDOW:SEQ].reshape(1, WINDOW, *kv_shape),
            states[1:][None],
            k_samp[:, -cache_rows:].reshape(DEC_BATCH, cache_rows, *kv_shape),
            v_samp[:, -cache_rows:].reshape(DEC_BATCH, cache_rows, *kv_shape))
```

```python
import functools

import jax
import jax.numpy as jnp
from jax import lax
from jax.experimental import pallas as pl
from jax.experimental.pallas import tpu as pltpu

F32 = jnp.float32
BF16 = jnp.bfloat16

D_MODEL = 4096
SEQ = 8192
DEC_BATCH = 8
DEC_SEQ = 64
PAST_LEN = 1024
CHUNK = 64
RMS_EPS = 1e-6
ROPE_THETA = 10000.0

GLA_HEADS = 8
GLA_DK = 256
GLA_DV = 512
GLA_GATE_RANK = 16
GLA_GATE_NORM = 16.0
QK_A = GLA_HEADS * GLA_DK
V_A = GLA_HEADS * GLA_DV
GLA_MAIN = 2 * QK_A + 2 * V_A

SWA_HEAD_DIM = 64
SWA_HEADS = 64
SWA_KV_HEADS = 8
SWA_GROUP = SWA_HEADS // SWA_KV_HEADS
WINDOW = 128
WIN_CHUNKS = WINDOW // CHUNK
KV_B = SWA_KV_HEADS * SWA_HEAD_DIM

FF_DIM = 14336
N_EXPERTS = 8
TOP_K = 2

T_TOK = SEQ + DEC_BATCH * DEC_SEQ
N_PROMPT_CHUNKS = SEQ // CHUNK
N_CHUNKS = T_TOK // CHUNK
LANE = 128

VMEM_LIMIT = 56 * 1024 * 1024

TM_TOK = 1088
MOE_SUB = 576
MOE_TILE = 4 * MOE_SUB
MOE_FUSED = 3
MOE_PIECE = 64


def _cparams(n_axes):
    return pltpu.CompilerParams(dimension_semantics=("arbitrary",) * n_axes,
                                vmem_limit_bytes=VMEM_LIMIT)


def _top2_route(logits, counts):
    rows = logits.shape[0]
    lane = lax.broadcasted_iota(jnp.int32, logits.shape, 1).astype(F32)
    neg = jnp.float32(-jnp.inf)
    lg = jnp.where(lane < N_EXPERTS, logits, neg)
    m1 = jnp.max(lg, axis=-1, keepdims=True)
    i1 = jnp.min(jnp.where(lg == m1, lane, float(LANE)), axis=-1, keepdims=True)
    lg2 = jnp.where(lane == i1, neg, lg)
    m2 = jnp.max(lg2, axis=-1, keepdims=True)
    i2 = jnp.min(jnp.where(lg2 == m2, lane, float(LANE)), axis=-1, keepdims=True)
    e2 = jnp.exp(m2 - m1)
    inv = 1.0 / (1.0 + e2)
    pick1 = lane == i1
    pick2 = lane == i2
    tri = (lax.broadcasted_iota(jnp.int32, (rows, rows), 0)
           >= lax.broadcasted_iota(jnp.int32, (rows, rows), 1)).astype(BF16)
    pre1 = jnp.dot(tri, pick1.astype(BF16), preferred_element_type=F32)
    pre2 = jnp.dot(tri, pick2.astype(BF16), preferred_element_type=F32)
    after1 = counts + pre1[rows - 1:rows, :]
    rank1 = jnp.sum(jnp.where(pick1, pre1 + counts, 0.0), axis=-1, keepdims=True) - 1.0
    rank2 = jnp.sum(jnp.where(pick2, pre2 + after1, 0.0), axis=-1, keepdims=True) - 1.0
    out = jnp.where(lane == 0, i1, jnp.where(lane == 1, i2, jnp.where(lane == 2, inv, e2 * inv)))
    out = jnp.where(lane == 4, rank1, jnp.where(lane == 5, rank2, out))
    return out, after1 + pre2[rows - 1:rows, :]


def _rmsnorm_body(*refs, n_out, has_delta, emit_sum, has_router):
    it = iter(refs)
    x_ref = next(it)
    d_ref = next(it) if has_delta else None
    g_ref = next(it)
    r_ref = next(it) if has_router else None
    sum_ref = next(it) if emit_sum else None
    o_refs = [next(it) for _ in range(n_out)]
    rt_ref = next(it) if has_router else None
    cnt_ref = next(it) if has_router else None

    if has_router:
        @pl.when(pl.program_id(0) == 0)
        def _():
            cnt_ref[...] = jnp.zeros_like(cnt_ref)

    x = x_ref[...]
    if has_delta:
        x = x + d_ref[...].astype(F32)
    if emit_sum:
        sum_ref[...] = x
    y = x * lax.rsqrt(jnp.mean(x * x, axis=-1, keepdims=True) + RMS_EPS)
    for i, o_ref in enumerate(o_refs):
        yi = y * g_ref[i:i + 1, :]
        if o_ref.dtype == jnp.uint32:
            o_ref[...] = _pack_halves(yi)
        else:
            o_ref[...] = yi.astype(o_ref.dtype)
        if has_router and i == 0:
            rw = r_ref[...]
            y_hi = yi.astype(BF16)
            y_lo = (yi - y_hi.astype(F32)).astype(BF16)
            r_hi = rw.astype(BF16)
            r_lo = (rw - r_hi.astype(F32)).astype(BF16)
            logits = (jnp.dot(y_hi, r_hi, preferred_element_type=F32)
                      + jnp.dot(y_lo, r_hi, preferred_element_type=F32)
                      + jnp.dot(y_hi, r_lo, preferred_element_type=F32))
            route, counts = _top2_route(logits, cnt_ref[0:1, :])
            rt_ref[...] = route
            cnt_ref[...] = jnp.broadcast_to(counts, cnt_ref.shape)


def _rmsnorm(x, gains, *, delta=None, emit_sum=False, router=None, out_dtype=BF16,
             row_off=0, n_rows=None, tr=256):
    n_rows = x.shape[0] if n_rows is None else n_rows
    d = x.shape[1]
    n_out = gains.shape[0]
    off = row_off // tr
    row_spec = pl.BlockSpec((tr, d), lambda i: (i + off, 0))
    out_spec = pl.BlockSpec((tr, d), lambda i: (i, 0))
    in_specs = [row_spec]
    args = [x]
    if delta is not None:
        in_specs.append(row_spec)
        args.append(delta)
    in_specs.append(pl.BlockSpec((n_out, d), lambda i: (0, 0)))
    args.append(gains)
    if router is not None:
        in_specs.append(pl.BlockSpec(router.shape, lambda i: (0, 0)))
        args.append(router)
    out_shape, out_specs = [], []
    if emit_sum:
        out_shape.append(jax.ShapeDtypeStruct((n_rows, d), F32))
        out_specs.append(out_spec)
    d_out = d // 2 if out_dtype == jnp.uint32 else d
    for _ in range(n_out):
        out_shape.append(jax.ShapeDtypeStruct((n_rows, d_out), out_dtype))
        out_specs.append(pl.BlockSpec((tr, d_out), lambda i: (i, 0)))
    if router is not None:
        out_shape.append(jax.ShapeDtypeStruct((n_rows, router.shape[1]), F32))
        out_specs.append(pl.BlockSpec((tr, router.shape[1]), lambda i: (i, 0)))
        out_shape.append(jax.ShapeDtypeStruct((8, router.shape[1]), F32))
        out_specs.append(pl.BlockSpec((8, router.shape[1]), lambda i: (0, 0)))
    body = functools.partial(_rmsnorm_body, n_out=n_out, has_delta=delta is not None,
                             emit_sum=emit_sum, has_router=router is not None)
    return pl.pallas_call(
        body, grid=(n_rows // tr,), in_specs=in_specs, out_specs=out_specs,
        out_shape=out_shape, compiler_params=_cparams(1), name="rmsnorm")(*args)


def _combine_norm_body(h_ref, y0_ref, y1_ref, rt_ref, g_ref, o_ref):
    rt = rt_ref[...]
    x = (h_ref[...] + y0_ref[...].astype(F32) * rt[:, TOP_K:TOP_K + 1]
         + y1_ref[...].astype(F32) * rt[:, TOP_K + 1:TOP_K + 2])
    y = x * lax.rsqrt(jnp.mean(x * x, axis=-1, keepdims=True) + RMS_EPS)
    o_ref[...] = y * g_ref[...]


def _combine_norm(h, y0, y1, route, gain, *, row_off, n_rows, tr=256):
    d = h.shape[1]
    off = row_off // tr
    row_spec = pl.BlockSpec((tr, d), lambda i: (i + off, 0))
    return pl.pallas_call(
        _combine_norm_body, grid=(n_rows // tr,),
        in_specs=[row_spec, row_spec, row_spec,
                  pl.BlockSpec((tr, route.shape[1]), lambda i: (i + off, 0)),
                  pl.BlockSpec((1, d), lambda i: (0, 0))],
        out_specs=pl.BlockSpec((tr, d), lambda i: (i, 0)),
        out_shape=jax.ShapeDtypeStruct((n_rows, d), F32),
        compiler_params=_cparams(1), name="combine_norm")(h, y0, y1, route, gain)


def _embed_norm_body(xp_ref, xs_ref, g_ref, x_ref, o_ref, *, n_prompt_tiles):
    def emit(x):
        x_ref[...] = x
        y = x * lax.rsqrt(jnp.mean(x * x, axis=-1, keepdims=True) + RMS_EPS)
        o_ref[...] = (y * g_ref[...]).astype(o_ref.dtype)

    @pl.when(pl.program_id(0) < n_prompt_tiles)
    def _():
        emit(xp_ref[...])

    @pl.when(pl.program_id(0) >= n_prompt_tiles)
    def _():
        emit(xs_ref[...])


def _embed_norm(x_prompt, x_sample, gain, *, tr=256):
    d = x_prompt.shape[1]
    n_p, n_s = x_prompt.shape[0] // tr, x_sample.shape[0] // tr
    out_spec = pl.BlockSpec((tr, d), lambda i: (i, 0))
    return pl.pallas_call(
        functools.partial(_embed_norm_body, n_prompt_tiles=n_p), grid=(n_p + n_s,),
        in_specs=[pl.BlockSpec((tr, d), lambda i: (jnp.minimum(i, n_p - 1), 0)),
                  pl.BlockSpec((tr, d), lambda i: (jnp.maximum(i - n_p, 0), 0)),
                  pl.BlockSpec((1, d), lambda i: (0, 0))],
        out_specs=[out_spec, out_spec],
        out_shape=[jax.ShapeDtypeStruct(((n_p + n_s) * tr, d), F32),
                   jax.ShapeDtypeStruct(((n_p + n_s) * tr, d), BF16)],
        compiler_params=_cparams(1), name="embed_norm")(x_prompt, x_sample, gain)


def _rope_tile(acc, cos_ref, sin_ref):
    cos = cos_ref[...]
    sin = sin_ref[...]
    lane = lax.broadcasted_iota(jnp.int32, (acc.shape[0], LANE), 1)
    first_half = (lane % SWA_HEAD_DIM) < (SWA_HEAD_DIM // 2)
    outs = []
    for j in range(acc.shape[1] // LANE):
        blk = acc[:, j * LANE:(j + 1) * LANE]
        rot = jnp.where(first_half,
                        pltpu.roll(blk, LANE - SWA_HEAD_DIM // 2, 1),
                        pltpu.roll(blk, SWA_HEAD_DIM // 2, 1))
        outs.append(blk * cos + rot * sin)
    return jnp.concatenate(outs, axis=1)


def _mm_body(*refs, has_bias, has_res, has_rope, w_rows_are_cols):
    it = iter(refs)
    x_ref = next(it)
    w_ref = next(it)
    b_ref = next(it) if has_bias else None
    cos_ref = next(it) if has_rope else None
    sin_ref = next(it) if has_rope else None
    res_ref = next(it) if has_res else None
    o_ref = next(it)
    wbf_ref = next(it)

    @pl.when(pl.program_id(1) == 0)
    def _():
        w = w_ref[...]
        wbf_ref[...] = (jnp.transpose(w) if w_rows_are_cols else w).astype(BF16)

    acc = jnp.dot(x_ref[...], wbf_ref[...], preferred_element_type=F32)
    if has_bias:
        acc = acc + b_ref[...]
    if has_rope:
        acc = _rope_tile(acc, cos_ref, sin_ref)
    if has_res:
        acc = acc + res_ref[...]
    o_ref[...] = acc.astype(o_ref.dtype)


def _mm(x, w, *, n_cols, tn, tm=TM_TOK, col_off=0, bias=None, res=None, rope=None,
        out_dtype=F32, w_rows_are_cols=False):
    m, k = x.shape
    cb = col_off // tn
    if w_rows_are_cols:
        w_spec = pl.BlockSpec((tn, k), lambda n, i: (n + cb, 0))
    else:
        w_spec = pl.BlockSpec((k, tn), lambda n, i: (0, n + cb))
    in_specs = [pl.BlockSpec((tm, k), lambda n, i: (i, 0)), w_spec]
    args = [x, w]
    if bias is not None:
        in_specs.append(pl.BlockSpec((1, tn), lambda n, i: (0, n + cb)))
        args.append(bias)
    if rope is not None:
        for t in rope:
            in_specs.append(pl.BlockSpec((tm, LANE), lambda n, i: (i, 0)))
            args.append(t)
    if res is not None:
        in_specs.append(pl.BlockSpec((tm, tn), lambda n, i: (i, n)))
        args.append(res)
    body = functools.partial(_mm_body, has_bias=bias is not None, has_res=res is not None,
                             has_rope=rope is not None, w_rows_are_cols=w_rows_are_cols)
    return pl.pallas_call(
        body, grid=(n_cols // tn, m // tm), in_specs=in_specs,
        out_specs=pl.BlockSpec((tm, tn), lambda n, i: (i, n)),
        out_shape=jax.ShapeDtypeStruct((m, n_cols), out_dtype),
        scratch_shapes=[pltpu.VMEM((k, tn), BF16)],
        compiler_params=_cparams(2), name="matmul_ws")(*args)


def _swiglu_tile(x, wg, wu):
    g = jnp.dot(x, wg, preferred_element_type=F32)
    u = jnp.dot(x, wu, preferred_element_type=F32)
    return g * jax.nn.sigmoid(g) * u


def _gateup_body(x_ref, wg_ref, wu_ref, o_ref, wgbf_ref, wubf_ref):
    @pl.when(pl.program_id(1) == 0)
    def _():
        wgbf_ref[...] = wg_ref[...].astype(BF16)
        wubf_ref[...] = wu_ref[...].astype(BF16)

    o_ref[...] = _swiglu_tile(x_ref[...], wgbf_ref[...], wubf_ref[...]).astype(o_ref.dtype)


def _gateup(x, wg, wu, *, tm=TM_TOK, tn=256):
    m, k = x.shape
    f = wg.shape[1]
    return pl.pallas_call(
        _gateup_body, grid=(f // tn, m // tm),
        in_specs=[pl.BlockSpec((tm, k), lambda n, i: (i, 0)),
                  pl.BlockSpec((k, tn), lambda n, i: (0, n)),
                  pl.BlockSpec((k, tn), lambda n, i: (0, n))],
        out_specs=pl.BlockSpec((tm, tn), lambda n, i: (i, n)),
        out_shape=jax.ShapeDtypeStruct((m, f), BF16),
        scratch_shapes=[pltpu.VMEM((k, tn), BF16), pltpu.VMEM((k, tn), BF16)],
        compiler_params=_cparams(2), name="ffn_gateup")(x, wg, wu)


def _for_row_pieces(nrows, fn, fn_first):
    nfull = nrows // MOE_SUB
    rem = nrows - nfull * MOE_SUB

    @pl.when(nfull >= MOE_FUSED)
    def _():
        fn_first(pl.ds(0, MOE_FUSED * MOE_SUB))

    for j in range(MOE_TILE // MOE_SUB):
        live = j < nfull
        if j < MOE_FUSED:
            live = jnp.logical_and(nfull < MOE_FUSED, live)

        @pl.when(live)
        def _():
            (fn_first if j == 0 else fn)(pl.ds(j * MOE_SUB, MOE_SUB))
    start = pl.multiple_of(nfull * MOE_SUB, MOE_PIECE)
    for m in range(MOE_PIECE, MOE_SUB + 1, MOE_PIECE):
        @pl.when(jnp.logical_and(rem > m - MOE_PIECE, rem <= m))
        def _():
            fn(pl.ds(start, m))


def _pack_halves(y):
    n = y.shape[1] // 2
    lo = lax.bitcast_convert_type(y[:, :n].astype(BF16).astype(F32), jnp.uint32) >> 16
    hi = lax.bitcast_convert_type(y[:, n:].astype(BF16).astype(F32), jnp.uint32)
    return (hi & jnp.uint32(0xFFFF0000)) | lo


def _unpack_halves(u):
    lo = lax.bitcast_convert_type(u << 16, F32).astype(BF16)
    hi = lax.bitcast_convert_type(u & jnp.uint32(0xFFFF0000), F32).astype(BF16)
    return jnp.concatenate([lo, hi], axis=1)


def _moe_gateup_body(te_ref, src_ref, nsub_ref, nrows_ref, tok_ref, xp_hbm, wg_ref, wu_ref,
                     o_ref, x_ref, stage_ref, wgbf_ref, wubf_ref, sem):
    r = pl.program_id(0)
    n = pl.program_id(1)
    nsub = nsub_ref[r]

    def row_copy(tok, i):
        return pltpu.make_async_copy(xp_hbm.at[pl.ds(tok, 1)], stage_ref.at[pl.ds(i, 1)], sem)

    @pl.when(jnp.logical_and(r == 0, n == 0))
    def _():
        stage_ref[...] = jnp.zeros_like(stage_ref)

    @pl.when(jnp.logical_and(n == 0, nsub > 0))
    def _():
        for j in range(MOE_TILE // MOE_SUB):
            @pl.when(j < nsub)
            def _():
                live = jnp.clip(nrows_ref[r] - j * MOE_SUB, 0, MOE_SUB)

                def start(i, carry):
                    row_copy(tok_ref[0, j * MOE_SUB + i], i).start()
                    return carry

                def wait(i, carry):
                    row_copy(0, i).wait()
                    return carry

                lax.fori_loop(0, live, start, 0)
                lax.fori_loop(0, live, wait, 0)
                x_ref[pl.ds(j * MOE_SUB, MOE_SUB), :] = _unpack_halves(stage_ref[...])

    @pl.when(nsub > 0)
    def _():
        nrows = nrows_ref[r]

        @pl.when(nrows < MOE_SUB)
        def _():
            wgbf_ref[...] = wg_ref[...].astype(BF16)
            wubf_ref[...] = wu_ref[...].astype(BF16)

        def first_piece(rows):
            wg = wg_ref[...].astype(BF16)
            wu = wu_ref[...].astype(BF16)
            wgbf_ref[...] = wg
            wubf_ref[...] = wu
            o_ref[rows, :] = _swiglu_tile(x_ref[rows, :], wg, wu).astype(o_ref.dtype)

        def piece(rows):
            o_ref[rows, :] = _swiglu_tile(x_ref[rows, :], wgbf_ref[...],
                                          wubf_ref[...]).astype(o_ref.dtype)

        _for_row_pieces(nrows, piece, first_piece)


def _moe_gateup(meta, slot_tok, xp, wg, wu, *, tn=256):
    n_tiles = slot_tok.shape[0]
    kp = xp.shape[1]
    k, f = wg.shape[1], wg.shape[2]
    nn = f // tn

    def col(n, r, ns):
        return jnp.where(ns[r] > 0, n, nn - 1)

    grid_spec = pltpu.PrefetchScalarGridSpec(
        num_scalar_prefetch=4, grid=(n_tiles, nn),
        in_specs=[pl.BlockSpec((None, 1, MOE_TILE), lambda r, n, te, src, ns, nr: (src[r], 0, 0),
                               memory_space=pltpu.SMEM),
                  pl.BlockSpec(memory_space=pl.ANY),
                  pl.BlockSpec((None, k, tn),
                               lambda r, n, te, src, ns, nr: (te[r], 0, col(n, r, ns))),
                  pl.BlockSpec((None, k, tn),
                               lambda r, n, te, src, ns, nr: (te[r], 0, col(n, r, ns)))],
        out_specs=pl.BlockSpec((MOE_TILE, tn),
                               lambda r, n, te, src, ns, nr: (src[r], col(n, r, ns))),
        scratch_shapes=[pltpu.VMEM((MOE_TILE, k), BF16), pltpu.VMEM((MOE_SUB, kp), jnp.uint32),
                        pltpu.VMEM((k, tn), BF16), pltpu.VMEM((k, tn), BF16),
                        pltpu.SemaphoreType.DMA(())])
    return pl.pallas_call(
        _moe_gateup_body, grid_spec=grid_spec,
        out_shape=jax.ShapeDtypeStruct((n_tiles * MOE_TILE, f), BF16),
        compiler_params=_cparams(2), name="moe_gateup")(*meta, slot_tok, xp, wg, wu)


def _down_body(a_ref, w_ref, res_ref, o_ref, acc_ref):
    kk = pl.program_id(2)

    @pl.when(kk == 0)
    def _():
        acc_ref[...] = jnp.zeros_like(acc_ref)

    acc_ref[...] += jnp.dot(a_ref[...], w_ref[...].astype(BF16), preferred_element_type=F32)

    @pl.when(kk == pl.num_programs(2) - 1)
    def _():
        o_ref[...] = (acc_ref[...] + res_ref[...]).astype(o_ref.dtype)


def _down(a, w, res, *, tm=TM_TOK, tn=1024, tk=2048):
    m, k = a.shape
    n = w.shape[1]
    return pl.pallas_call(
        _down_body, grid=(m // tm, n // tn, k // tk),
        in_specs=[pl.BlockSpec((tm, tk), lambda i, j, kk: (i, kk)),
                  pl.BlockSpec((tk, tn), lambda i, j, kk: (kk, j)),
                  pl.BlockSpec((tm, tn), lambda i, j, kk: (i, j))],
        out_specs=pl.BlockSpec((tm, tn), lambda i, j, kk: (i, j)),
        out_shape=jax.ShapeDtypeStruct((m, n), F32),
        scratch_shapes=[pltpu.VMEM((tm, tn), F32)],
        compiler_params=_cparams(3), name="ffn_down")(a, w, res)


def _moe_down_body(te_ref, src_ref, nsub_ref, nrows_ref, a_ref, w_ref, o_ref, acc_ref, wbf_ref):
    r = pl.program_id(0)
    kk = pl.program_id(2)
    nsub = nsub_ref[r]

    @pl.when(nsub > 0)
    def _():
        @pl.when(kk == 0)
        def _():
            acc_ref[...] = jnp.zeros_like(acc_ref)

        nrows = nrows_ref[r]

        @pl.when(nrows < MOE_SUB)
        def _():
            wbf_ref[...] = w_ref[...].astype(BF16)

        def first_piece(rows):
            w = w_ref[...].astype(BF16)
            wbf_ref[...] = w
            acc_ref[rows, :] += jnp.dot(a_ref[rows, :], w, preferred_element_type=F32)

        def piece(rows):
            acc_ref[rows, :] += jnp.dot(a_ref[rows, :], wbf_ref[...],
                                        preferred_element_type=F32)

        _for_row_pieces(nrows, piece, first_piece)

        @pl.when(kk == pl.num_programs(2) - 1)
        def _():
            o_ref[...] = acc_ref[...].astype(o_ref.dtype)


def _moe_down(meta, a, w, *, tn=1024, tk=1024):
    n_slots, k = a.shape
    n = w.shape[2]
    nn, nk = n // tn, k // tk

    def a_map(r, j, kk, te, src, ns, nr):
        return (src[r], jnp.where(ns[r] > 0, kk, nk - 1))

    def w_map(r, j, kk, te, src, ns, nr):
        live = ns[r] > 0
        return (te[r], jnp.where(live, kk, nk - 1), jnp.where(live, j, nn - 1))

    def o_map(r, j, kk, te, src, ns, nr):
        return (src[r], jnp.where(ns[r] > 0, j, nn - 1))

    grid_spec = pltpu.PrefetchScalarGridSpec(
        num_scalar_prefetch=4, grid=(n_slots // MOE_TILE, nn, nk),
        in_specs=[pl.BlockSpec((MOE_TILE, tk), a_map), pl.BlockSpec((None, tk, tn), w_map)],
        out_specs=pl.BlockSpec((MOE_TILE, tn), o_map),
        scratch_shapes=[pltpu.VMEM((MOE_TILE, tn), F32), pltpu.VMEM((tk, tn), BF16)])
    return pl.pallas_call(
        _moe_down_body, grid_spec=grid_spec,
        out_shape=jax.ShapeDtypeStruct((n_slots, n), BF16),
        compiler_params=_cparams(3), name="moe_down")(*meta, a, w)


def _gla_body(q_ref, k_ref, v_ref, g_ref, a_ref, wa_ref, ba_ref, gn_ref, s0_ref,
              o_ref, s_ref):
    c = pl.program_id(0)

    @pl.when(c == 0)
    def _():
        s_ref[...] = jnp.zeros_like(s_ref)

    @pl.when(c >= N_PROMPT_CHUNKS)
    def _():
        s_ref[...] = s0_ref[...]

    xg = jnp.dot(a_ref[...].astype(BF16), wa_ref[...].astype(BF16),
                 preferred_element_type=F32) + ba_ref[...]
    la = (jnp.minimum(xg, 0.0) - jnp.log1p(jnp.exp(-jnp.abs(xg)))) * (1.0 / GLA_GATE_NORM)

    row = lax.broadcasted_iota(jnp.int32, (CHUNK, CHUNK), 0)
    col = lax.broadcasted_iota(jnp.int32, (CHUNK, CHUNK), 1)
    causal = row >= col
    tri = causal.astype(BF16)
    la_hi = la.astype(BF16)
    la_lo = (la - la_hi.astype(F32)).astype(BF16)
    cum = (jnp.dot(tri, la_hi, preferred_element_type=F32)
           + jnp.dot(tri, la_lo, preferred_element_type=F32))
    last = cum[CHUNK - 1:CHUNK, :]

    q = q_ref[...].astype(F32) * (GLA_DK ** -0.5)
    q_in = q * jnp.exp(cum)
    k_out = (k_ref[...].astype(F32) * jnp.exp(last - cum)).astype(BF16)
    q_loc = (q_in * jnp.exp(-last)).astype(BF16)
    q_in = q_in.astype(BF16)
    dec_rows = jnp.broadcast_to(jnp.exp(last), (LANE, QK_A))

    for h in range(GLA_HEADS):
        dk = slice(h * GLA_DK, (h + 1) * GLA_DK)
        dv = slice(h * GLA_DV, (h + 1) * GLA_DV)
        v = v_ref[:, dv]
        g = g_ref[:, dv].astype(F32)
        s_old = s_ref[h]
        attn = lax.dot_general(q_loc[:, dk], k_out[:, dk], (((1,), (1,)), ((), ())),
                               preferred_element_type=F32)
        attn = jnp.where(causal, attn, 0.0)
        o = (jnp.dot(q_in[:, dk], s_old.astype(BF16), preferred_element_type=F32)
             + jnp.dot(attn.astype(BF16), v, preferred_element_type=F32))
        kv = lax.dot_general(k_out[:, dk], v, (((0,), (0,)), ((), ())),
                             preferred_element_type=F32)
        dec_col = jnp.transpose(dec_rows[:, dk])
        dec = jnp.concatenate([dec_col] * (GLA_DV // LANE), axis=1)
        s_ref[h] = dec * s_old + kv
        on = o * lax.rsqrt(jnp.mean(o * o, axis=-1, keepdims=True) + RMS_EPS) * gn_ref[...]
        o_ref[:, dv] = (on * (g * jax.nn.sigmoid(g))).astype(o_ref.dtype)


def _gla(proj, a_low, w_a2p, b_a, g_norm, s0):
    n_seq = 1 + DEC_BATCH
    state_block = (None, GLA_HEADS, GLA_DK, GLA_DV)

    def s0_map(c):
        return (jnp.maximum(c - N_PROMPT_CHUNKS, 0), 0, 0, 0)

    def s_map(c):
        return (jnp.where(c < N_PROMPT_CHUNKS, 0, c - N_PROMPT_CHUNKS + 1), 0, 0, 0)

    return pl.pallas_call(
        _gla_body, grid=(N_CHUNKS,),
        in_specs=[pl.BlockSpec((CHUNK, QK_A), lambda c: (c, 0)),
                  pl.BlockSpec((CHUNK, QK_A), lambda c: (c, 1)),
                  pl.BlockSpec((CHUNK, V_A), lambda c: (c, 1)),
                  pl.BlockSpec((CHUNK, V_A), lambda c: (c, 2)),
                  pl.BlockSpec((CHUNK, LANE), lambda c: (c, 0)),
                  pl.BlockSpec((LANE, QK_A), lambda c: (0, 0)),
                  pl.BlockSpec((1, QK_A), lambda c: (0, 0)),
                  pl.BlockSpec((1, GLA_DV), lambda c: (0, 0)),
                  pl.BlockSpec(state_block, s0_map)],
        out_specs=[pl.BlockSpec((CHUNK, V_A), lambda c: (c, 0)),
                   pl.BlockSpec(state_block, s_map)],
        out_shape=[jax.ShapeDtypeStruct((T_TOK, V_A), BF16),
                   jax.ShapeDtypeStruct((n_seq, GLA_HEADS, GLA_DK, GLA_DV), F32)],
        compiler_params=_cparams(1), name="gla")(
            proj, proj, proj, proj, a_low, w_a2p, b_a, g_norm, s0)


def _swa_body(q_ref, k0_ref, k1_ref, k2_ref, v0_ref, v1_ref, v2_ref, sink_ref, o_ref):
    c = pl.program_id(0)
    kcat = jnp.concatenate([k0_ref[...], k1_ref[...], k2_ref[...]], axis=0)
    vcat = jnp.concatenate([v0_ref[...], v1_ref[...], v2_ref[...]], axis=0)
    n_keys = (WIN_CHUNKS + 1) * CHUNK
    n_pairs = SWA_GROUP // 2
    n_q = n_pairs * CHUNK
    key_chunk = lax.broadcasted_iota(jnp.int32, (n_keys, n_q), 0) // CHUNK
    valid = jnp.logical_or(c >= N_PROMPT_CHUNKS, c - WIN_CHUNKS + key_chunk >= 0)
    low_half = lax.broadcasted_iota(jnp.int32, (n_keys, LANE), 1) < SWA_HEAD_DIM
    scale = SWA_HEAD_DIM ** -0.5

    for slab in range(KV_B // LANE):
        ks = kcat[:, slab * LANE:(slab + 1) * LANE]
        vs = vcat[:, slab * LANE:(slab + 1) * LANE]
        ks_sw = pltpu.roll(ks, SWA_HEAD_DIM, 1)
        vs_sw = pltpu.roll(vs, SWA_HEAD_DIM, 1)
        for par in range(2):
            kh = 2 * slab + par
            k_lo, k_hi = (ks, ks_sw) if par == 0 else (ks_sw, ks)
            v_lo, v_hi = (vs, vs_sw) if par == 0 else (vs_sw, vs)
            k_half = (jnp.where(low_half, k_lo, 0.0).astype(BF16),
                      jnp.where(low_half, 0.0, k_hi).astype(BF16))
            v_half = (jnp.where(low_half, v_lo, 0.0).astype(BF16),
                      jnp.where(low_half, 0.0, v_hi).astype(BF16))
            first = kh * n_pairs
            qs = jnp.concatenate([q_ref[:, (first + p) * LANE:(first + p + 1) * LANE]
                                  for p in range(n_pairs)], axis=0)
            acc = None
            for parity in range(2):
                s = lax.dot_general(k_half[parity], qs, (((1,), (1,)), ((), ())),
                                    preferred_element_type=F32) * scale
                s = jnp.where(valid, s, -1e30)
                sink = sink_ref[kh, parity][0:1, :]
                m = jnp.maximum(jnp.max(s, axis=0, keepdims=True), sink)
                p = jnp.exp(s - m)
                denom = jnp.sum(p, axis=0, keepdims=True) + jnp.exp(sink - m)
                pn = (p * (1.0 / denom)).astype(BF16)
                o = lax.dot_general(pn, v_half[parity], (((0,), (0,)), ((), ())),
                                    preferred_element_type=F32)
                acc = o if acc is None else acc + o
            for p in range(n_pairs):
                o_ref[:, (first + p) * LANE:(first + p + 1) * LANE] = (
                    acc[p * CHUNK:(p + 1) * CHUNK, :].astype(o_ref.dtype))


def _swa(q, kf, vf, sink_rows):
    def kv_map(j):
        def index(c):
            prompt = jnp.maximum(c - WIN_CHUNKS + j, 0)
            sample = N_PROMPT_CHUNKS + (WIN_CHUNKS + 1) * (c - N_PROMPT_CHUNKS) + j
            return (jnp.where(c < N_PROMPT_CHUNKS, prompt, sample), 0)
        return index

    kv_specs = [pl.BlockSpec((CHUNK, KV_B), kv_map(j)) for j in range(WIN_CHUNKS + 1)]
    return pl.pallas_call(
        _swa_body, grid=(N_CHUNKS,),
        in_specs=[pl.BlockSpec((CHUNK, D_MODEL), lambda c: (c, 0))] + kv_specs + kv_specs
        + [pl.BlockSpec(sink_rows.shape, lambda c: (0, 0, 0, 0))],
        out_specs=pl.BlockSpec((CHUNK, D_MODEL), lambda c: (c, 0)),
        out_shape=jax.ShapeDtypeStruct((T_TOK, D_MODEL), BF16),
        compiler_params=_cparams(1), name="swa")(q, kf, kf, kf, vf, vf, vf, sink_rows)


def _route(route, counts):
    n_tok = route.shape[0]
    flat_e = route[:, 0:TOP_K].astype(jnp.int32).reshape(-1)
    rank = route[:, 2 * TOP_K:3 * TOP_K].astype(jnp.int32).reshape(-1)
    counts = counts[0, :N_EXPERTS].astype(jnp.int32)
    n_assign = n_tok * TOP_K
    n_tiles_e = (counts + MOE_TILE - 1) // MOE_TILE
    tile_end = jnp.cumsum(n_tiles_e)
    tile_start = tile_end - n_tiles_e
    dest = tile_start[flat_e] * MOE_TILE + rank
    n_tiles = -(-n_assign // MOE_TILE) + N_EXPERTS
    n_slots = n_tiles * MOE_TILE
    slot_tok = jnp.zeros((n_slots,), jnp.int32).at[dest].set(
        jnp.arange(n_assign, dtype=jnp.int32) // TOP_K)
    n_used = tile_end[-1]
    r = jnp.arange(n_tiles, dtype=jnp.int32)
    src = jnp.maximum(jnp.minimum(r, n_used - 1), 0)
    te = jnp.clip(jnp.searchsorted(tile_end, src, side='right'), 0, N_EXPERTS - 1).astype(jnp.int32)
    nrows = jnp.where(r < n_used,
                      jnp.clip(counts[te] - (r - tile_start[te]) * MOE_TILE, 0, MOE_TILE), 0)
    nsub = (nrows + MOE_SUB - 1) // MOE_SUB
    meta = (te, src.astype(jnp.int32), nsub.astype(jnp.int32), nrows.astype(jnp.int32))
    return meta, slot_tok.reshape(n_tiles, 1, MOE_TILE), dest.reshape(n_tok, TOP_K)


def kernel(x_prompt, x_sample, state_gla, cache_swa_k, cache_swa_v, ln_mix, ln_ff, ln_kv, ln_out, gla_w_in, gla_w_a2, gla_b_a, gla_norm, gla_w_o, kv_w, kv_b, swa_w_q, swa_b_q, swa_sinks, swa_w_o, swa_b_o, ffd_w_gate, ffd_w_up, ffd_w_down, moe_router, moe_w_gate, moe_w_up, moe_w_down):
    x, hn = _embed_norm(x_prompt.reshape(SEQ, D_MODEL),
                        x_sample.reshape(DEC_BATCH * DEC_SEQ, D_MODEL), ln_mix[0:1])
    w_in_t = jnp.swapaxes(gla_w_in[0], 0, 1)
    proj = _mm(hn, w_in_t, n_cols=GLA_MAIN, tn=512, out_dtype=BF16, w_rows_are_cols=True)
    w_low_t = jnp.pad(w_in_t[GLA_MAIN:, :], ((0, LANE - GLA_GATE_RANK), (0, 0)))
    a_low = _mm(hn, w_low_t, n_cols=LANE, tn=LANE, out_dtype=F32, w_rows_are_cols=True)
    w_a2p = jnp.pad(gla_w_a2[0], ((0, LANE - GLA_GATE_RANK), (0, 0)))
    o_gla, states = _gla(proj, a_low, w_a2p, gla_b_a[0:1], gla_norm[0:1], state_gla[0])
    h = _mm(o_gla, gla_w_o[0], n_cols=D_MODEL, tn=512, res=x, out_dtype=F32)

    (hn,) = _rmsnorm(h, ln_ff[0:1])
    act = _gateup(hn, ffd_w_gate[0], ffd_w_up[0])
    h = _down(act, ffd_w_down[0], h)

    hkv, hn = _rmsnorm(h, jnp.stack([ln_kv, ln_mix[1]]))
    pos = jnp.concatenate([jnp.arange(SEQ, dtype=jnp.int32),
                           jnp.tile(PAST_LEN + jnp.arange(DEC_SEQ, dtype=jnp.int32), DEC_BATCH)])
    half = SWA_HEAD_DIM // 2
    inv_freq = ROPE_THETA ** (-jnp.arange(half, dtype=F32) / half)
    ang = pos.astype(F32)[:, None] * inv_freq[None, :]
    cos, sin = jnp.cos(ang), jnp.sin(ang)
    rope = (jnp.concatenate([cos, cos, cos, cos], axis=1),
            jnp.concatenate([-sin, sin, -sin, sin], axis=1))
    kv_bias = kv_b[None, :]
    k_new = _mm(hkv, kv_w, n_cols=KV_B, tn=512, bias=kv_bias, rope=rope, out_dtype=F32)
    v_new = _mm(hkv, kv_w, n_cols=KV_B, tn=512, col_off=KV_B, bias=kv_bias, out_dtype=F32)
    q = _mm(hn, swa_w_q[0], n_cols=D_MODEL, tn=512, bias=swa_b_q[0:1], rope=rope,
            out_dtype=BF16)

    cache_rows = cache_swa_k.shape[1]

    def with_cache(new, cache):
        return jnp.concatenate([cache.reshape(DEC_BATCH, cache_rows, KV_B),
                                new[SEQ:].reshape(DEC_BATCH, DEC_SEQ, KV_B)], axis=1)

    k_samp = with_cache(k_new, cache_swa_k)
    v_samp = with_cache(v_new, cache_swa_v)
    kf = jnp.concatenate([k_new[:SEQ], k_samp.reshape(-1, KV_B)], axis=0)
    vf = jnp.concatenate([v_new[:SEQ], v_samp.reshape(-1, KV_B)], axis=0)
    sinks = swa_sinks[0].reshape(SWA_KV_HEADS, SWA_GROUP // 2, 2).transpose(0, 2, 1)
    sink_rows = jnp.broadcast_to(
        jnp.repeat(sinks, CHUNK, axis=2)[:, :, None, :],
        (SWA_KV_HEADS, 2, 8, (SWA_GROUP // 2) * CHUNK))
    o_swa = _swa(q, kf, vf, sink_rows)
    h = _mm(o_swa, swa_w_o[0], n_cols=D_MODEL, tn=512, bias=swa_b_o[0:1], res=h, out_dtype=F32)

    router = jnp.pad(moe_router[0], ((0, 0), (0, LANE - N_EXPERTS)))
    hn_packed, route, counts = _rmsnorm(h, ln_ff[1:2], router=router, out_dtype=jnp.uint32)
    meta, slot_tok, dest = _route(route, counts)
    act = _moe_gateup(meta, slot_tok, hn_packed, moe_w_gate[0], moe_w_up[0])
    yb = _moe_down(meta, act, moe_w_down[0])
    y0, y1 = yb[dest[:, 0]], yb[dest[:, 1]]
    y_prompt = _combine_norm(h, y0, y1, route, ln_out[None, :], row_off=0, n_rows=SEQ)
    y_sample = _combine_norm(h, y0, y1, route, ln_out[None, :], row_off=SEQ,
                             n_rows=DEC_BATCH * DEC_SEQ)

    kv_shape = (SWA_KV_HEADS, SWA_HEAD_DIM)
    return (y_prompt.reshape(1, SEQ, D_MODEL),
            y_sample.reshape(DEC_BATCH, DEC_SEQ, D_MODEL),
            states[0:1][None],
            k_new[SEQ - WINDOW:SEQ].reshape(1, WINDOW, *kv_shape),
            v_new[SEQ - WINDOW:SEQ].reshape(1, WINDOW, *kv_shape),
            states[1:][None],
            k_samp[:, -cache_rows:].reshape(DEC_BATCH, cache_rows, *kv_shape),
            v_samp[:, -cache_rows:].reshape(DEC_BATCH, cache_rows, *kv_shape))
```

```python
import functools

import jax
import jax.numpy as jnp
from jax import lax
from jax.experimental import pallas as pl
from jax.experimental.pallas import tpu as pltpu

F32 = jnp.float32
BF16 = jnp.bfloat16

D_MODEL = 4096
SEQ = 8192
DEC_BATCH = 8
DEC_SEQ = 64
PAST_LEN = 1024
CHUNK = 64
RMS_EPS = 1e-6
ROPE_THETA = 10000.0

GLA_HEADS = 8
GLA_DK = 256
GLA_DV = 512
GLA_GATE_RANK = 16
GLA_GATE_NORM = 16.0
QK_A = GLA_HEADS * GLA_DK
V_A = GLA_HEADS * GLA_DV
GLA_MAIN = 2 * QK_A + 2 * V_A

SWA_HEAD_DIM = 64
SWA_HEADS = 64
SWA_KV_HEADS = 8
SWA_GROUP = SWA_HEADS // SWA_KV_HEADS
WINDOW = 128
WIN_CHUNKS = WINDOW // CHUNK
KV_B = SWA_KV_HEADS * SWA_HEAD_DIM

FF_DIM = 14336
N_EXPERTS = 8
TOP_K = 2

T_TOK = SEQ + DEC_BATCH * DEC_SEQ
N_PROMPT_CHUNKS = SEQ // CHUNK
N_CHUNKS = T_TOK // CHUNK
LANE = 128

VMEM_LIMIT = 56 * 1024 * 1024

TM_TOK = 1088
MOE_SUB = 576
MOE_TILE = 4 * MOE_SUB
MOE_FUSED = 3
MOE_PIECE = 64


def _cparams(n_axes):
    return pltpu.CompilerParams(dimension_semantics=("arbitrary",) * n_axes,
                                vmem_limit_bytes=VMEM_LIMIT)


def _top2_route(logits, counts):
    rows = logits.shape[0]
    lane = lax.broadcasted_iota(jnp.int32, logits.shape, 1).astype(F32)
    neg = jnp.float32(-jnp.inf)
    lg = jnp.where(lane < N_EXPERTS, logits, neg)
    m1 = jnp.max(lg, axis=-1, keepdims=True)
    i1 = jnp.min(jnp.where(lg == m1, lane, float(LANE)), axis=-1, keepdims=True)
    lg2 = jnp.where(lane == i1, neg, lg)
    m2 = jnp.max(lg2, axis=-1, keepdims=True)
    i2 = jnp.min(jnp.where(lg2 == m2, lane, float(LANE)), axis=-1, keepdims=True)
    e2 = jnp.exp(m2 - m1)
    inv = 1.0 / (1.0 + e2)
    pick1 = lane == i1
    pick2 = lane == i2
    tri = (lax.broadcasted_iota(jnp.int32, (rows, rows), 0)
           >= lax.broadcasted_iota(jnp.int32, (rows, rows), 1)).astype(BF16)
    pre1 = jnp.dot(tri, pick1.astype(BF16), preferred_element_type=F32)
    pre2 = jnp.dot(tri, pick2.astype(BF16), preferred_element_type=F32)
    after1 = counts + pre1[rows - 1:rows, :]
    rank1 = jnp.sum(jnp.where(pick1, pre1 + counts, 0.0), axis=-1, keepdims=True) - 1.0
    rank2 = jnp.sum(jnp.where(pick2, pre2 + after1, 0.0), axis=-1, keepdims=True) - 1.0
    out = jnp.where(lane == 0, i1, jnp.where(lane == 1, i2, jnp.where(lane == 2, inv, e2 * inv)))
    out = jnp.where(lane == 4, rank1, jnp.where(lane == 5, rank2, out))
    return out, after1 + pre2[rows - 1:rows, :]


def _rmsnorm_body(*refs, n_out, has_delta, emit_sum, has_router):
    it = iter(refs)
    x_ref = next(it)
    d_ref = next(it) if has_delta else None
    g_ref = next(it)
    r_ref = next(it) if has_router else None
    sum_ref = next(it) if emit_sum else None
    o_refs = [next(it) for _ in range(n_out)]
    rt_ref = next(it) if has_router else None
    cnt_ref = next(it) if has_router else None

    if has_router:
        @pl.when(pl.program_id(0) == 0)
        def _():
            cnt_ref[...] = jnp.zeros_like(cnt_ref)

    x = x_ref[...]
    if has_delta:
        x = x + d_ref[...].astype(F32)
    if emit_sum:
        sum_ref[...] = x
    y = x * lax.rsqrt(jnp.mean(x * x, axis=-1, keepdims=True) + RMS_EPS)
    for i, o_ref in enumerate(o_refs):
        yi = y * g_ref[i:i + 1, :]
        if o_ref.dtype == jnp.uint32:
            o_ref[...] = _pack_halves(yi)
        else:
            o_ref[...] = yi.astype(o_ref.dtype)
        if has_router and i == 0:
            rw = r_ref[...]
            y_hi = yi.astype(BF16)
            y_lo = (yi - y_hi.astype(F32)).astype(BF16)
            r_hi = rw.astype(BF16)
            r_lo = (rw - r_hi.astype(F32)).astype(BF16)
            logits = (jnp.dot(y_hi, r_hi, preferred_element_type=F32)
                      + jnp.dot(y_lo, r_hi, preferred_element_type=F32)
                      + jnp.dot(y_hi, r_lo, preferred_element_type=F32))
            route, counts = _top2_route(logits, cnt_ref[0:1, :])
            rt_ref[...] = route
            cnt_ref[...] = jnp.broadcast_to(counts, cnt_ref.shape)


def _rmsnorm(x, gains, *, delta=None, emit_sum=False, router=None, out_dtype=BF16,
             row_off=0, n_rows=None, tr=256):
    n_rows = x.shape[0] if n_rows is None else n_rows
    d = x.shape[1]
    n_out = gains.shape[0]
    off = row_off // tr
    row_spec = pl.BlockSpec((tr, d), lambda i: (i + off, 0))
    out_spec = pl.BlockSpec((tr, d), lambda i: (i, 0))
    in_specs = [row_spec]
    args = [x]
    if delta is not None:
        in_specs.append(row_spec)
        args.append(delta)
    in_specs.append(pl.BlockSpec((n_out, d), lambda i: (0, 0)))
    args.append(gains)
    if router is not None:
        in_specs.append(pl.BlockSpec(router.shape, lambda i: (0, 0)))
        args.append(router)
    out_shape, out_specs = [], []
    if emit_sum:
        out_shape.append(jax.ShapeDtypeStruct((n_rows, d), F32))
        out_specs.append(out_spec)
    d_out = d // 2 if out_dtype == jnp.uint32 else d
    for _ in range(n_out):
        out_shape.append(jax.ShapeDtypeStruct((n_rows, d_out), out_dtype))
        out_specs.append(pl.BlockSpec((tr, d_out), lambda i: (i, 0)))
    if router is not None:
        out_shape.append(jax.ShapeDtypeStruct((n_rows, router.shape[1]), F32))
        out_specs.append(pl.BlockSpec((tr, router.shape[1]), lambda i: (i, 0)))
        out_shape.append(jax.ShapeDtypeStruct((8, router.shape[1]), F32))
        out_specs.append(pl.BlockSpec((8, router.shape[1]), lambda i: (0, 0)))
    body = functools.partial(_rmsnorm_body, n_out=n_out, has_delta=delta is not None,
                             emit_sum=emit_sum, has_router=router is not None)
    return pl.pallas_call(
        body, grid=(n_rows // tr,), in_specs=in_specs, out_specs=out_specs,
        out_shape=out_shape, compiler_params=_cparams(1), name="rmsnorm")(*args)


def _combine_norm_body(h_ref, y0_ref, y1_ref, rt_ref, g_ref, o_ref):
    rt = rt_ref[...]
    x = (h_ref[...] + y0_ref[...].astype(F32) * rt[:, TOP_K:TOP_K + 1]
         + y1_ref[...].astype(F32) * rt[:, TOP_K + 1:TOP_K + 2])
    y = x * lax.rsqrt(jnp.mean(x * x, axis=-1, keepdims=True) + RMS_EPS)
    o_ref[...] = y * g_ref[...]


def _combine_norm(h, y0, y1, route, gain, *, row_off, n_rows, tr=256):
    d = h.shape[1]
    off = row_off // tr
    row_spec = pl.BlockSpec((tr, d), lambda i: (i + off, 0))
    return pl.pallas_call(
        _combine_norm_body, grid=(n_rows // tr,),
        in_specs=[row_spec, row_spec, row_spec,
                  pl.BlockSpec((tr, route.shape[1]), lambda i: (i + off, 0)),
                  pl.BlockSpec((1, d), lambda i: (0, 0))],
        out_specs=pl.BlockSpec((tr, d), lambda i: (i, 0)),
        out_shape=jax.ShapeDtypeStruct((n_rows, d), F32),
        compiler_params=_cparams(1), name="combine_norm")(h, y0, y1, route, gain)


def _embed_norm_body(xp_ref, xs_ref, g_ref, x_ref, o_ref, *, n_prompt_tiles):
    def emit(x):
        x_ref[...] = x
        y = x * lax.rsqrt(jnp.mean(x * x, axis=-1, keepdims=True) + RMS_EPS)
        o_ref[...] = (y * g_ref[...]).astype(o_ref.dtype)

    @pl.when(pl.program_id(0) < n_prompt_tiles)
    def _():
        emit(xp_ref[...])

    @pl.when(pl.program_id(0) >= n_prompt_tiles)
    def _():
        emit(xs_ref[...])


def _embed_norm(x_prompt, x_sample, gain, *, tr=256):
    d = x_prompt.shape[1]
    n_p, n_s = x_prompt.shape[0] // tr, x_sample.shape[0] // tr
    out_spec = pl.BlockSpec((tr, d), lambda i: (i, 0))
    return pl.pallas_call(
        functools.partial(_embed_norm_body, n_prompt_tiles=n_p), grid=(n_p + n_s,),
        in_specs=[pl.BlockSpec((tr, d), lambda i: (jnp.minimum(i, n_p - 1), 0)),
                  pl.BlockSpec((tr, d), lambda i: (jnp.maximum(i - n_p, 0), 0)),
                  pl.BlockSpec((1, d), lambda i: (0, 0))],
        out_specs=[out_spec, out_spec],
        out_shape=[jax.ShapeDtypeStruct(((n_p + n_s) * tr, d), F32),
                   jax.ShapeDtypeStruct(((n_p + n_s) * tr, d), BF16)],
        compiler_params=_cparams(1), name="embed_norm")(x_prompt, x_sample, gain)


def _rope_tile(acc, cos_ref, sin_ref):
    cos = cos_ref[...]
    sin = sin_ref[...]
    lane = lax.broadcasted_iota(jnp.int32, (acc.shape[0], LANE), 1)
    first_half = (lane % SWA_HEAD_DIM) < (SWA_HEAD_DIM // 2)
    outs = []
    for j in range(acc.shape[1] // LANE):
        blk = acc[:, j * LANE:(j + 1) * LANE]
        rot = jnp.where(first_half,
                        pltpu.roll(blk, LANE - SWA_HEAD_DIM // 2, 1),
                        pltpu.roll(blk, SWA_HEAD_DIM // 2, 1))
        outs.append(blk * cos + rot * sin)
    return jnp.concatenate(outs, axis=1)


def _mm_body(*refs, has_bias, has_res, has_rope, w_rows_are_cols):
    it = iter(refs)
    x_ref = next(it)
    w_ref = next(it)
    b_ref = next(it) if has_bias else None
    cos_ref = next(it) if has_rope else None
    sin_ref = next(it) if has_rope else None
    res_ref = next(it) if has_res else None
    o_ref = next(it)
    wbf_ref = next(it)

    @pl.when(pl.program_id(1) == 0)
    def _():
        w = w_ref[...]
        wbf_ref[...] = (jnp.transpose(w) if w_rows_are_cols else w).astype(BF16)

    acc = jnp.dot(x_ref[...], wbf_ref[...], preferred_element_type=F32)
    if has_bias:
        acc = acc + b_ref[...]
    if has_rope:
        acc = _rope_tile(acc, cos_ref, sin_ref)
    if has_res:
        acc = acc + res_ref[...]
    o_ref[...] = acc.astype(o_ref.dtype)


def _mm(x, w, *, n_cols, tn, tm=TM_TOK, col_off=0, bias=None, res=None, rope=None,
        out_dtype=F32, w_rows_are_cols=False):
    m, k = x.shape
    cb = col_off // tn
    if w_rows_are_cols:
        w_spec = pl.BlockSpec((tn, k), lambda n, i: (n + cb, 0))
    else:
        w_spec = pl.BlockSpec((k, tn), lambda n, i: (0, n + cb))
    in_specs = [pl.BlockSpec((tm, k), lambda n, i: (i, 0)), w_spec]
    args = [x, w]
    if bias is not None:
        in_specs.append(pl.BlockSpec((1, tn), lambda n, i: (0, n + cb)))
        args.append(bias)
    if rope is not None:
        for t in rope:
            in_specs.append(pl.BlockSpec((tm, LANE), lambda n, i: (i, 0)))
            args.append(t)
    if res is not None:
        in_specs.append(pl.BlockSpec((tm, tn), lambda n, i: (i, n)))
        args.append(res)
    body = functools.partial(_mm_body, has_bias=bias is not None, has_res=res is not None,
                             has_rope=rope is not None, w_rows_are_cols=w_rows_are_cols)
    return pl.pallas_call(
        body, grid=(n_cols // tn, m // tm), in_specs=in_specs,
        out_specs=pl.BlockSpec((tm, tn), lambda n, i: (i, n)),
        out_shape=jax.ShapeDtypeStruct((m, n_cols), out_dtype),
        scratch_shapes=[pltpu.VMEM((k, tn), BF16)],
        compiler_params=_cparams(2), name="matmul_ws")(*args)


def _swiglu_tile(x, wg, wu):
    g = jnp.dot(x, wg, preferred_element_type=F32)
    u = jnp.dot(x, wu, preferred_element_type=F32)
    return g * jax.nn.sigmoid(g) * u


def _gateup_body(x_ref, wg_ref, wu_ref, o_ref, wgbf_ref, wubf_ref):
    @pl.when(pl.program_id(1) == 0)
    def _():
        wgbf_ref[...] = wg_ref[...].astype(BF16)
        wubf_ref[...] = wu_ref[...].astype(BF16)

    o_ref[...] = _swiglu_tile(x_ref[...], wgbf_ref[...], wubf_ref[...]).astype(o_ref.dtype)


def _gateup(x, wg, wu, *, tm=TM_TOK, tn=256):
    m, k = x.shape
    f = wg.shape[1]
    return pl.pallas_call(
        _gateup_body, grid=(f // tn, m // tm),
        in_specs=[pl.BlockSpec((tm, k), lambda n, i: (i, 0)),
                  pl.BlockSpec((k, tn), lambda n, i: (0, n)),
                  pl.BlockSpec((k, tn), lambda n, i: (0, n))],
        out_specs=pl.BlockSpec((tm, tn), lambda n, i: (i, n)),
        out_shape=jax.ShapeDtypeStruct((m, f), BF16),
        scratch_shapes=[pltpu.VMEM((k, tn), BF16), pltpu.VMEM((k, tn), BF16)],
        compiler_params=_cparams(2), name="ffn_gateup")(x, wg, wu)


def _for_row_pieces(nrows, fn, fn_first):
    nfull = nrows // MOE_SUB
    rem = nrows - nfull * MOE_SUB

    @pl.when(nfull >= MOE_FUSED)
    def _():
        fn_first(pl.ds(0, MOE_FUSED * MOE_SUB))

    for j in range(MOE_TILE // MOE_SUB):
        live = j < nfull
        if j < MOE_FUSED:
            live = jnp.logical_and(nfull < MOE_FUSED, live)

        @pl.when(live)
        def _():
            (fn_first if j == 0 else fn)(pl.ds(j * MOE_SUB, MOE_SUB))
    start = pl.multiple_of(nfull * MOE_SUB, MOE_PIECE)
    for m in range(MOE_PIECE, MOE_SUB + 1, MOE_PIECE):
        @pl.when(jnp.logical_and(rem > m - MOE_PIECE, rem <= m))
        def _():
            fn(pl.ds(start, m))


def _pack_halves(y):
    n = y.shape[1] // 2
    lo = lax.bitcast_convert_type(y[:, :n].astype(BF16).astype(F32), jnp.uint32) >> 16
    hi = lax.bitcast_convert_type(y[:, n:].astype(BF16).astype(F32), jnp.uint32)
    return (hi & jnp.uint32(0xFFFF0000)) | lo


def _unpack_halves(u):
    lo = lax.bitcast_convert_type(u << 16, F32).astype(BF16)
    hi = lax.bitcast_convert_type(u & jnp.uint32(0xFFFF0000), F32).astype(BF16)
    return jnp.concatenate([lo, hi], axis=1)


def _moe_gateup_body(te_ref, src_ref, nsub_ref, nrows_ref, tok_ref, xp_hbm, wg_ref, wu_ref,
                     o_ref, x_ref, stage_ref, wgbf_ref, wubf_ref, sem):
    r = pl.program_id(0)
    n = pl.program_id(1)
    nsub = nsub_ref[r]

    def row_copy(tok, i):
        return pltpu.make_async_copy(xp_hbm.at[pl.ds(tok, 1)], stage_ref.at[pl.ds(i, 1)], sem)

    @pl.when(jnp.logical_and(r == 0, n == 0))
    def _():
        stage_ref[...] = jnp.zeros_like(stage_ref)

    @pl.when(jnp.logical_and(n == 0, nsub > 0))
    def _():
        for j in range(MOE_TILE // MOE_SUB):
            @pl.when(j < nsub)
            def _():
                live = jnp.clip(nrows_ref[r] - j * MOE_SUB, 0, MOE_SUB)

                def start(i, carry):
                    row_copy(tok_ref[0, j * MOE_SUB + i], i).start()
                    return carry

                def wait(i, carry):
                    row_copy(0, i).wait()
                    return carry

                @pl.when(live == MOE_SUB)
                def _():
                    lax.fori_loop(0, MOE_SUB, start, 0, unroll=8)
                    lax.fori_loop(0, MOE_SUB, wait, 0, unroll=8)

                @pl.when(live < MOE_SUB)
                def _():
                    lax.fori_loop(0, live, start, 0)
                    lax.fori_loop(0, live, wait, 0)

                x_ref[pl.ds(j * MOE_SUB, MOE_SUB), :] = _unpack_halves(stage_ref[...])

    @pl.when(nsub > 0)
    def _():
        nrows = nrows_ref[r]

        @pl.when(nrows < MOE_SUB)
        def _():
            wgbf_ref[...] = wg_ref[...].astype(BF16)
            wubf_ref[...] = wu_ref[...].astype(BF16)

        def first_piece(rows):
            wg = wg_ref[...].astype(BF16)
            wu = wu_ref[...].astype(BF16)
            wgbf_ref[...] = wg
            wubf_ref[...] = wu
            o_ref[rows, :] = _swiglu_tile(x_ref[rows, :], wg, wu).astype(o_ref.dtype)

        def piece(rows):
            o_ref[rows, :] = _swiglu_tile(x_ref[rows, :], wgbf_ref[...],
                                          wubf_ref[...]).astype(o_ref.dtype)

        _for_row_pieces(nrows, piece, first_piece)


def _moe_gateup(meta, slot_tok, xp, wg, wu, *, tn=256):
    n_tiles = slot_tok.shape[0]
    kp = xp.shape[1]
    k, f = wg.shape[1], wg.shape[2]
    nn = f // tn

    def col(n, r, ns):
        return jnp.where(ns[r] > 0, n, nn - 1)

    grid_spec = pltpu.PrefetchScalarGridSpec(
        num_scalar_prefetch=4, grid=(n_tiles, nn),
        in_specs=[pl.BlockSpec((None, 1, MOE_TILE), lambda r, n, te, src, ns, nr: (src[r], 0, 0),
                               memory_space=pltpu.SMEM),
                  pl.BlockSpec(memory_space=pl.ANY),
                  pl.BlockSpec((None, k, tn),
                               lambda r, n, te, src, ns, nr: (te[r], 0, col(n, r, ns))),
                  pl.BlockSpec((None, k, tn),
                               lambda r, n, te, src, ns, nr: (te[r], 0, col(n, r, ns)))],
        out_specs=pl.BlockSpec((MOE_TILE, tn),
                               lambda r, n, te, src, ns, nr: (src[r], col(n, r, ns))),
        scratch_shapes=[pltpu.VMEM((MOE_TILE, k), BF16), pltpu.VMEM((MOE_SUB, kp), jnp.uint32),
                        pltpu.VMEM((k, tn), BF16), pltpu.VMEM((k, tn), BF16),
                        pltpu.SemaphoreType.DMA(())])
    return pl.pallas_call(
        _moe_gateup_body, grid_spec=grid_spec,
        out_shape=jax.ShapeDtypeStruct((n_tiles * MOE_TILE, f), BF16),
        compiler_params=_cparams(2), name="moe_gateup")(*meta, slot_tok, xp, wg, wu)


def _down_body(a_ref, w_ref, res_ref, o_ref, acc_ref):
    kk = pl.program_id(2)

    @pl.when(kk == 0)
    def _():
        acc_ref[...] = jnp.zeros_like(acc_ref)

    acc_ref[...] += jnp.dot(a_ref[...], w_ref[...].astype(BF16), preferred_element_type=F32)

    @pl.when(kk == pl.num_programs(2) - 1)
    def _():
        o_ref[...] = (acc_ref[...] + res_ref[...]).astype(o_ref.dtype)


def _down(a, w, res, *, tm=TM_TOK, tn=1024, tk=2048):
    m, k = a.shape
    n = w.shape[1]
    return pl.pallas_call(
        _down_body, grid=(m // tm, n // tn, k // tk),
        in_specs=[pl.BlockSpec((tm, tk), lambda i, j, kk: (i, kk)),
                  pl.BlockSpec((tk, tn), lambda i, j, kk: (kk, j)),
                  pl.BlockSpec((tm, tn), lambda i, j, kk: (i, j))],
        out_specs=pl.BlockSpec((tm, tn), lambda i, j, kk: (i, j)),
        out_shape=jax.ShapeDtypeStruct((m, n), F32),
        scratch_shapes=[pltpu.VMEM((tm, tn), F32)],
        compiler_params=_cparams(3), name="ffn_down")(a, w, res)


def _moe_down_body(te_ref, src_ref, nsub_ref, nrows_ref, a_ref, w_ref, o_ref, acc_ref, wbf_ref):
    r = pl.program_id(0)
    kk = pl.program_id(2)
    nsub = nsub_ref[r]

    @pl.when(nsub > 0)
    def _():
        @pl.when(kk == 0)
        def _():
            acc_ref[...] = jnp.zeros_like(acc_ref)

        nrows = nrows_ref[r]

        @pl.when(nrows < MOE_SUB)
        def _():
            wbf_ref[...] = w_ref[...].astype(BF16)

        def first_piece(rows):
            w = w_ref[...].astype(BF16)
            wbf_ref[...] = w
            acc_ref[rows, :] += jnp.dot(a_ref[rows, :], w, preferred_element_type=F32)

        def piece(rows):
            acc_ref[rows, :] += jnp.dot(a_ref[rows, :], wbf_ref[...],
                                        preferred_element_type=F32)

        _for_row_pieces(nrows, piece, first_piece)

        @pl.when(kk == pl.num_programs(2) - 1)
        def _():
            o_ref[...] = acc_ref[...].astype(o_ref.dtype)


def _moe_down(meta, a, w, *, tn=1024, tk=1024):
    n_slots, k = a.shape
    n = w.shape[2]
    nn, nk = n // tn, k // tk

    def a_map(r, j, kk, te, src, ns, nr):
        return (src[r], jnp.where(ns[r] > 0, kk, nk - 1))

    def w_map(r, j, kk, te, src, ns, nr):
        live = ns[r] > 0
        return (te[r], jnp.where(live, kk, nk - 1), jnp.where(live, j, nn - 1))

    def o_map(r, j, kk, te, src, ns, nr):
        return (src[r], jnp.where(ns[r] > 0, j, nn - 1))

    grid_spec = pltpu.PrefetchScalarGridSpec(
        num_scalar_prefetch=4, grid=(n_slots // MOE_TILE, nn, nk),
        in_specs=[pl.BlockSpec((MOE_TILE, tk), a_map), pl.BlockSpec((None, tk, tn), w_map)],
        out_specs=pl.BlockSpec((MOE_TILE, tn), o_map),
        scratch_shapes=[pltpu.VMEM((MOE_TILE, tn), F32), pltpu.VMEM((tk, tn), BF16)])
    return pl.pallas_call(
        _moe_down_body, grid_spec=grid_spec,
        out_shape=jax.ShapeDtypeStruct((n_slots, n), BF16),
        compiler_params=_cparams(3), name="moe_down")(*meta, a, w)


def _gla_body(q_ref, k_ref, v_ref, g_ref, a_ref, wa_ref, ba_ref, gn_ref, s0_ref,
              o_ref, s_ref):
    c = pl.program_id(0)

    @pl.when(c == 0)
    def _():
        s_ref[...] = jnp.zeros_like(s_ref)

    @pl.when(c >= N_PROMPT_CHUNKS)
    def _():
        s_ref[...] = s0_ref[...]

    xg = jnp.dot(a_ref[...].astype(BF16), wa_ref[...].astype(BF16),
                 preferred_element_type=F32) + ba_ref[...]
    la = (jnp.minimum(xg, 0.0) - jnp.log1p(jnp.exp(-jnp.abs(xg)))) * (1.0 / GLA_GATE_NORM)

    row = lax.broadcasted_iota(jnp.int32, (CHUNK, CHUNK), 0)
    col = lax.broadcasted_iota(jnp.int32, (CHUNK, CHUNK), 1)
    causal = row >= col
    tri = causal.astype(BF16)
    la_hi = la.astype(BF16)
    la_lo = (la - la_hi.astype(F32)).astype(BF16)
    cum = (jnp.dot(tri, la_hi, preferred_element_type=F32)
           + jnp.dot(tri, la_lo, preferred_element_type=F32))
    last = cum[CHUNK - 1:CHUNK, :]

    q = q_ref[...].astype(F32) * (GLA_DK ** -0.5)
    q_in = q * jnp.exp(cum)
    k_out = (k_ref[...].astype(F32) * jnp.exp(last - cum)).astype(BF16)
    q_loc = (q_in * jnp.exp(-last)).astype(BF16)
    q_in = q_in.astype(BF16)
    dec_rows = jnp.broadcast_to(jnp.exp(last), (LANE, QK_A))

    for h in range(GLA_HEADS):
        dk = slice(h * GLA_DK, (h + 1) * GLA_DK)
        dv = slice(h * GLA_DV, (h + 1) * GLA_DV)
        v = v_ref[:, dv]
        g = g_ref[:, dv].astype(F32)
        s_old = s_ref[h]
        attn = lax.dot_general(q_loc[:, dk], k_out[:, dk], (((1,), (1,)), ((), ())),
                               preferred_element_type=F32)
        attn = jnp.where(causal, attn, 0.0)
        o = (jnp.dot(q_in[:, dk], s_old.astype(BF16), preferred_element_type=F32)
             + jnp.dot(attn.astype(BF16), v, preferred_element_type=F32))
        kv = lax.dot_general(k_out[:, dk], v, (((0,), (0,)), ((), ())),
                             preferred_element_type=F32)
        dec_col = jnp.transpose(dec_rows[:, dk])
        dec = jnp.concatenate([dec_col] * (GLA_DV // LANE), axis=1)
        s_ref[h] = dec * s_old + kv
        on = o * lax.rsqrt(jnp.mean(o * o, axis=-1, keepdims=True) + RMS_EPS) * gn_ref[...]
        o_ref[:, dv] = (on * (g * jax.nn.sigmoid(g))).astype(o_ref.dtype)


def _gla(proj, a_low, w_a2p, b_a, g_norm, s0):
    n_seq = 1 + DEC_BATCH
    state_block = (None, GLA_HEADS, GLA_DK, GLA_DV)

    def s0_map(c):
        return (jnp.maximum(c - N_PROMPT_CHUNKS, 0), 0, 0, 0)

    def s_map(c):
        return (jnp.where(c < N_PROMPT_CHUNKS, 0, c - N_PROMPT_CHUNKS + 1), 0, 0, 0)

    return pl.pallas_call(
        _gla_body, grid=(N_CHUNKS,),
        in_specs=[pl.BlockSpec((CHUNK, QK_A), lambda c: (c, 0)),
                  pl.BlockSpec((CHUNK, QK_A), lambda c: (c, 1)),
                  pl.BlockSpec((CHUNK, V_A), lambda c: (c, 1)),
                  pl.BlockSpec((CHUNK, V_A), lambda c: (c, 2)),
                  pl.BlockSpec((CHUNK, LANE), lambda c: (c, 0)),
                  pl.BlockSpec((LANE, QK_A), lambda c: (0, 0)),
                  pl.BlockSpec((1, QK_A), lambda c: (0, 0)),
                  pl.BlockSpec((1, GLA_DV), lambda c: (0, 0)),
                  pl.BlockSpec(state_block, s0_map)],
        out_specs=[pl.BlockSpec((CHUNK, V_A), lambda c: (c, 0)),
                   pl.BlockSpec(state_block, s_map)],
        out_shape=[jax.ShapeDtypeStruct((T_TOK, V_A), BF16),
                   jax.ShapeDtypeStruct((n_seq, GLA_HEADS, GLA_DK, GLA_DV), F32)],
        compiler_params=_cparams(1), name="gla")(
            proj, proj, proj, proj, a_low, w_a2p, b_a, g_norm, s0)


def _swa_body(q_ref, k0_ref, k1_ref, k2_ref, v0_ref, v1_ref, v2_ref, sink_ref, o_ref):
    c = pl.program_id(0)
    kcat = jnp.concatenate([k0_ref[...], k1_ref[...], k2_ref[...]], axis=0)
    vcat = jnp.concatenate([v0_ref[...], v1_ref[...], v2_ref[...]], axis=0)
    n_keys = (WIN_CHUNKS + 1) * CHUNK
    n_pairs = SWA_GROUP // 2
    n_q = n_pairs * CHUNK
    key_chunk = lax.broadcasted_iota(jnp.int32, (n_keys, n_q), 0) // CHUNK
    valid = jnp.logical_or(c >= N_PROMPT_CHUNKS, c - WIN_CHUNKS + key_chunk >= 0)
    low_half = lax.broadcasted_iota(jnp.int32, (n_keys, LANE), 1) < SWA_HEAD_DIM
    scale = SWA_HEAD_DIM ** -0.5

    for slab in range(KV_B // LANE):
        ks = kcat[:, slab * LANE:(slab + 1) * LANE]
        vs = vcat[:, slab * LANE:(slab + 1) * LANE]
        ks_sw = pltpu.roll(ks, SWA_HEAD_DIM, 1)
        vs_sw = pltpu.roll(vs, SWA_HEAD_DIM, 1)
        for par in range(2):
            kh = 2 * slab + par
            k_lo, k_hi = (ks, ks_sw) if par == 0 else (ks_sw, ks)
            v_lo, v_hi = (vs, vs_sw) if par == 0 else (vs_sw, vs)
            k_half = (jnp.where(low_half, k_lo, 0.0).astype(BF16),
                      jnp.where(low_half, 0.0, k_hi).astype(BF16))
            v_half = (jnp.where(low_half, v_lo, 0.0).astype(BF16),
                      jnp.where(low_half, 0.0, v_hi).astype(BF16))
            first = kh * n_pairs
            qs = jnp.concatenate([q_ref[:, (first + p) * LANE:(first + p + 1) * LANE]
                                  for p in range(n_pairs)], axis=0)
            acc = None
            for parity in range(2):
                s = lax.dot_general(k_half[parity], qs, (((1,), (1,)), ((), ())),
                                    preferred_element_type=F32) * scale
                s = jnp.where(valid, s, -1e30)
                sink = sink_ref[kh, parity][0:1, :]
                m = jnp.maximum(jnp.max(s, axis=0, keepdims=True), sink)
                p = jnp.exp(s - m)
                denom = jnp.sum(p, axis=0, keepdims=True) + jnp.exp(sink - m)
                pn = (p * (1.0 / denom)).astype(BF16)
                o = lax.dot_general(pn, v_half[parity], (((0,), (0,)), ((), ())),
                                    preferred_element_type=F32)
                acc = o if acc is None else acc + o
            for p in range(n_pairs):
                o_ref[:, (first + p) * LANE:(first + p + 1) * LANE] = (
                    acc[p * CHUNK:(p + 1) * CHUNK, :].astype(o_ref.dtype))


def _swa(q, kf, vf, sink_rows):
    def kv_map(j):
        def index(c):
            prompt = jnp.maximum(c - WIN_CHUNKS + j, 0)
            sample = N_PROMPT_CHUNKS + (WIN_CHUNKS + 1) * (c - N_PROMPT_CHUNKS) + j
            return (jnp.where(c < N_PROMPT_CHUNKS, prompt, sample), 0)
        return index

    kv_specs = [pl.BlockSpec((CHUNK, KV_B), kv_map(j)) for j in range(WIN_CHUNKS + 1)]
    return pl.pallas_call(
        _swa_body, grid=(N_CHUNKS,),
        in_specs=[pl.BlockSpec((CHUNK, D_MODEL), lambda c: (c, 0))] + kv_specs + kv_specs
        + [pl.BlockSpec(sink_rows.shape, lambda c: (0, 0, 0, 0))],
        out_specs=pl.BlockSpec((CHUNK, D_MODEL), lambda c: (c, 0)),
        out_shape=jax.ShapeDtypeStruct((T_TOK, D_MODEL), BF16),
        compiler_params=_cparams(1), name="swa")(q, kf, kf, kf, vf, vf, vf, sink_rows)


def _route(route, counts):
    n_tok = route.shape[0]
    flat_e = route[:, 0:TOP_K].astype(jnp.int32).reshape(-1)
    rank = route[:, 2 * TOP_K:3 * TOP_K].astype(jnp.int32).reshape(-1)
    counts = counts[0, :N_EXPERTS].astype(jnp.int32)
    n_assign = n_tok * TOP_K
    n_tiles_e = (counts + MOE_TILE - 1) // MOE_TILE
    tile_end = jnp.cumsum(n_tiles_e)
    tile_start = tile_end - n_tiles_e
    dest = tile_start[flat_e] * MOE_TILE + rank
    n_tiles = -(-n_assign // MOE_TILE) + N_EXPERTS
    n_slots = n_tiles * MOE_TILE
    slot_tok = jnp.zeros((n_slots,), jnp.int32).at[dest].set(
        jnp.arange(n_assign, dtype=jnp.int32) // TOP_K)
    n_used = tile_end[-1]
    r = jnp.arange(n_tiles, dtype=jnp.int32)
    src = jnp.maximum(jnp.minimum(r, n_used - 1), 0)
    te = jnp.clip(jnp.searchsorted(tile_end, src, side='right'), 0, N_EXPERTS - 1).astype(jnp.int32)
    nrows = jnp.where(r < n_used,
                      jnp.clip(counts[te] - (r - tile_start[te]) * MOE_TILE, 0, MOE_TILE), 0)
    nsub = (nrows + MOE_SUB - 1) // MOE_SUB
    meta = (te, src.astype(jnp.int32), nsub.astype(jnp.int32), nrows.astype(jnp.int32))
    return meta, slot_tok.reshape(n_tiles, 1, MOE_TILE), dest.reshape(n_tok, TOP_K)


def kernel(x_prompt, x_sample, state_gla, cache_swa_k, cache_swa_v, ln_mix, ln_ff, ln_kv, ln_out, gla_w_in, gla_w_a2, gla_b_a, gla_norm, gla_w_o, kv_w, kv_b, swa_w_q, swa_b_q, swa_sinks, swa_w_o, swa_b_o, ffd_w_gate, ffd_w_up, ffd_w_down, moe_router, moe_w_gate, moe_w_up, moe_w_down):
    x, hn = _embed_norm(x_prompt.reshape(SEQ, D_MODEL),
                        x_sample.reshape(DEC_BATCH * DEC_SEQ, D_MODEL), ln_mix[0:1])
    w_in_t = jnp.swapaxes(gla_w_in[0], 0, 1)
    proj = _mm(hn, w_in_t, n_cols=GLA_MAIN, tn=512, out_dtype=BF16, w_rows_are_cols=True)
    w_low_t = jnp.pad(w_in_t[GLA_MAIN:, :], ((0, LANE - GLA_GATE_RANK), (0, 0)))
    a_low = _mm(hn, w_low_t, n_cols=LANE, tn=LANE, out_dtype=F32, w_rows_are_cols=True)
    w_a2p = jnp.pad(gla_w_a2[0], ((0, LANE - GLA_GATE_RANK), (0, 0)))
    o_gla, states = _gla(proj, a_low, w_a2p, gla_b_a[0:1], gla_norm[0:1], state_gla[0])
    h = _mm(o_gla, gla_w_o[0], n_cols=D_MODEL, tn=512, res=x, out_dtype=F32)

    (hn,) = _rmsnorm(h, ln_ff[0:1])
    act = _gateup(hn, ffd_w_gate[0], ffd_w_up[0])
    h = _down(act, ffd_w_down[0], h)

    hkv, hn = _rmsnorm(h, jnp.stack([ln_kv, ln_mix[1]]))
    pos = jnp.concatenate([jnp.arange(SEQ, dtype=jnp.int32),
                           jnp.tile(PAST_LEN + jnp.arange(DEC_SEQ, dtype=jnp.int32), DEC_BATCH)])
    half = SWA_HEAD_DIM // 2
    inv_freq = ROPE_THETA ** (-jnp.arange(half, dtype=F32) / half)
    ang = pos.astype(F32)[:, None] * inv_freq[None, :]
    cos, sin = jnp.cos(ang), jnp.sin(ang)
    rope = (jnp.concatenate([cos, cos, cos, cos], axis=1),
            jnp.concatenate([-sin, sin, -sin, sin], axis=1))
    kv_bias = kv_b[None, :]
    k_new = _mm(hkv, kv_w, n_cols=KV_B, tn=512, bias=kv_bias, rope=rope, out_dtype=F32)
    v_new = _mm(hkv, kv_w, n_cols=KV_B, tn=512, col_off=KV_B, bias=kv_bias, out_dtype=F32)
    q = _mm(hn, swa_w_q[0], n_cols=D_MODEL, tn=512, bias=swa_b_q[0:1], rope=rope,
            out_dtype=BF16)

    cache_rows = cache_swa_k.shape[1]

    def with_cache(new, cache):
        return jnp.concatenate([cache.reshape(DEC_BATCH, cache_rows, KV_B),
                                new[SEQ:].reshape(DEC_BATCH, DEC_SEQ, KV_B)], axis=1)

    k_samp = with_cache(k_new, cache_swa_k)
    v_samp = with_cache(v_new, cache_swa_v)
    kf = jnp.concatenate([k_new[:SEQ], k_samp.reshape(-1, KV_B)], axis=0)
    vf = jnp.concatenate([v_new[:SEQ], v_samp.reshape(-1, KV_B)], axis=0)
    sinks = swa_sinks[0].reshape(SWA_KV_HEADS, SWA_GROUP // 2, 2).transpose(0, 2, 1)
    sink_rows = jnp.broadcast_to(
        jnp.repeat(sinks, CHUNK, axis=2)[:, :, None, :],
        (SWA_KV_HEADS, 2, 8, (SWA_GROUP // 2) * CHUNK))
    o_swa = _swa(q, kf, vf, sink_rows)
    h = _mm(o_swa, swa_w_o[0], n_cols=D_MODEL, tn=512, bias=swa_b_o[0:1], res=h, out_dtype=F32)

    router = jnp.pad(moe_router[0], ((0, 0), (0, LANE - N_EXPERTS)))
    hn_packed, route, counts = _rmsnorm(h, ln_ff[1:2], router=router, out_dtype=jnp.uint32)
    meta, slot_tok, dest = _route(route, counts)
    act = _moe_gateup(meta, slot_tok, hn_packed, moe_w_gate[0], moe_w_up[0])
    yb = _moe_down(meta, act, moe_w_down[0])
    y0, y1 = yb[dest[:, 0]], yb[dest[:, 1]]
    y_prompt = _combine_norm(h, y0, y1, route, ln_out[None, :], row_off=0, n_rows=SEQ)
    y_sample = _combine_norm(h, y0, y1, route, ln_out[None, :], row_off=SEQ,
                             n_rows=DEC_BATCH * DEC_SEQ)

    kv_shape = (SWA_KV_HEADS, SWA_HEAD_DIM)
    return (y_prompt.reshape(1, SEQ, D_MODEL),
            y_sample.reshape(DEC_BATCH, DEC_SEQ, D_MODEL),
            states[0:1][None],
            k_new[SEQ - WINDOW:SEQ].reshape(1, WINDOW, *kv_shape),
            v_new[SEQ - WINDOW:SEQ].reshape(1, WINDOW, *kv_shape),
            states[1:][None],
            k_samp[:, -cache_rows:].reshape(DEC_BATCH, cache_rows, *kv_shape),
            v_samp[:, -cache_rows:].reshape(DEC_BATCH, cache_rows, *kv_shape))
```

```python
import functools

import jax
import jax.numpy as jnp
from jax import lax
from jax.experimental import pallas as pl
from jax.experimental.pallas import tpu as pltpu

F32 = jnp.float32
BF16 = jnp.bfloat16

D_MODEL = 4096
SEQ = 8192
DEC_BATCH = 8
DEC_SEQ = 64
PAST_LEN = 1024
CHUNK = 64
RMS_EPS = 1e-6
ROPE_THETA = 10000.0

GLA_HEADS = 8
GLA_DK = 256
GLA_DV = 512
GLA_GATE_RANK = 16
GLA_GATE_NORM = 16.0
QK_A = GLA_HEADS * GLA_DK
V_A = GLA_HEADS * GLA_DV
GLA_MAIN = 2 * QK_A + 2 * V_A

SWA_HEAD_DIM = 64
SWA_HEADS = 64
SWA_KV_HEADS = 8
SWA_GROUP = SWA_HEADS // SWA_KV_HEADS
WINDOW = 128
WIN_CHUNKS = WINDOW // CHUNK
KV_B = SWA_KV_HEADS * SWA_HEAD_DIM

FF_DIM = 14336
N_EXPERTS = 8
TOP_K = 2

T_TOK = SEQ + DEC_BATCH * DEC_SEQ
N_PROMPT_CHUNKS = SEQ // CHUNK
N_CHUNKS = T_TOK // CHUNK
LANE = 128

VMEM_LIMIT = 56 * 1024 * 1024

TM_TOK = 1088
MOE_SUB = 576
MOE_TILE = 4 * MOE_SUB
MOE_FUSED = 3
MOE_PIECE = 64


def _cparams(n_axes):
    return pltpu.CompilerParams(dimension_semantics=("arbitrary",) * n_axes,
                                vmem_limit_bytes=VMEM_LIMIT)


def _top2_route(logits, counts):
    rows = logits.shape[0]
    lane = lax.broadcasted_iota(jnp.int32, logits.shape, 1).astype(F32)
    neg = jnp.float32(-jnp.inf)
    lg = jnp.where(lane < N_EXPERTS, logits, neg)
    m1 = jnp.max(lg, axis=-1, keepdims=True)
    i1 = jnp.min(jnp.where(lg == m1, lane, float(LANE)), axis=-1, keepdims=True)
    lg2 = jnp.where(lane == i1, neg, lg)
    m2 = jnp.max(lg2, axis=-1, keepdims=True)
    i2 = jnp.min(jnp.where(lg2 == m2, lane, float(LANE)), axis=-1, keepdims=True)
    e2 = jnp.exp(m2 - m1)
    inv = 1.0 / (1.0 + e2)
    pick1 = lane == i1
    pick2 = lane == i2
    tri = (lax.broadcasted_iota(jnp.int32, (rows, rows), 0)
           >= lax.broadcasted_iota(jnp.int32, (rows, rows), 1)).astype(BF16)
    pre1 = jnp.dot(tri, pick1.astype(BF16), preferred_element_type=F32)
    pre2 = jnp.dot(tri, pick2.astype(BF16), preferred_element_type=F32)
    after1 = counts + pre1[rows - 1:rows, :]
    rank1 = jnp.sum(jnp.where(pick1, pre1 + counts, 0.0), axis=-1, keepdims=True) - 1.0
    rank2 = jnp.sum(jnp.where(pick2, pre2 + after1, 0.0), axis=-1, keepdims=True) - 1.0
    out = jnp.where(lane == 0, i1, jnp.where(lane == 1, i2, jnp.where(lane == 2, inv, e2 * inv)))
    out = jnp.where(lane == 4, rank1, jnp.where(lane == 5, rank2, out))
    return out, after1 + pre2[rows - 1:rows, :]


def _rmsnorm_body(*refs, n_out, has_delta, emit_sum, has_router):
    it = iter(refs)
    x_ref = next(it)
    d_ref = next(it) if has_delta else None
    g_ref = next(it)
    r_ref = next(it) if has_router else None
    sum_ref = next(it) if emit_sum else None
    o_refs = [next(it) for _ in range(n_out)]
    rt_ref = next(it) if has_router else None
    cnt_ref = next(it) if has_router else None

    if has_router:
        @pl.when(pl.program_id(0) == 0)
        def _():
            cnt_ref[...] = jnp.zeros_like(cnt_ref)

    x = x_ref[...]
    if has_delta:
        x = x + d_ref[...].astype(F32)
    if emit_sum:
        sum_ref[...] = x
    y = x * lax.rsqrt(jnp.mean(x * x, axis=-1, keepdims=True) + RMS_EPS)
    for i, o_ref in enumerate(o_refs):
        yi = y * g_ref[i:i + 1, :]
        if o_ref.dtype == jnp.uint32:
            o_ref[...] = _pack_halves(yi)
        else:
            o_ref[...] = yi.astype(o_ref.dtype)
        if has_router and i == 0:
            rw = r_ref[...]
            y_hi = yi.astype(BF16)
            y_lo = (yi - y_hi.astype(F32)).astype(BF16)
            r_hi = rw.astype(BF16)
            r_lo = (rw - r_hi.astype(F32)).astype(BF16)
            logits = (jnp.dot(y_hi, r_hi, preferred_element_type=F32)
                      + jnp.dot(y_lo, r_hi, preferred_element_type=F32)
                      + jnp.dot(y_hi, r_lo, preferred_element_type=F32))
            route, counts = _top2_route(logits, cnt_ref[0:1, :])
            rt_ref[...] = route
            cnt_ref[...] = jnp.broadcast_to(counts, cnt_ref.shape)


def _rmsnorm(x, gains, *, delta=None, emit_sum=False, router=None, out_dtype=BF16,
             row_off=0, n_rows=None, tr=512):
    n_rows = x.shape[0] if n_rows is None else n_rows
    d = x.shape[1]
    n_out = gains.shape[0]
    off = row_off // tr
    row_spec = pl.BlockSpec((tr, d), lambda i: (i + off, 0))
    out_spec = pl.BlockSpec((tr, d), lambda i: (i, 0))
    in_specs = [row_spec]
    args = [x]
    if delta is not None:
        in_specs.append(row_spec)
        args.append(delta)
    in_specs.append(pl.BlockSpec((n_out, d), lambda i: (0, 0)))
    args.append(gains)
    if router is not None:
        in_specs.append(pl.BlockSpec(router.shape, lambda i: (0, 0)))
        args.append(router)
    out_shape, out_specs = [], []
    if emit_sum:
        out_shape.append(jax.ShapeDtypeStruct((n_rows, d), F32))
        out_specs.append(out_spec)
    d_out = d // 2 if out_dtype == jnp.uint32 else d
    for _ in range(n_out):
        out_shape.append(jax.ShapeDtypeStruct((n_rows, d_out), out_dtype))
        out_specs.append(pl.BlockSpec((tr, d_out), lambda i: (i, 0)))
    if router is not None:
        out_shape.append(jax.ShapeDtypeStruct((n_rows, router.shape[1]), F32))
        out_specs.append(pl.BlockSpec((tr, router.shape[1]), lambda i: (i, 0)))
        out_shape.append(jax.ShapeDtypeStruct((8, router.shape[1]), F32))
        out_specs.append(pl.BlockSpec((8, router.shape[1]), lambda i: (0, 0)))
    body = functools.partial(_rmsnorm_body, n_out=n_out, has_delta=delta is not None,
                             emit_sum=emit_sum, has_router=router is not None)
    return pl.pallas_call(
        body, grid=(n_rows // tr,), in_specs=in_specs, out_specs=out_specs,
        out_shape=out_shape, compiler_params=_cparams(1), name="rmsnorm")(*args)


def _combine_norm_body(h_ref, y0_ref, y1_ref, rt_ref, g_ref, o_ref):
    rt = rt_ref[...]
    x = (h_ref[...] + y0_ref[...].astype(F32) * rt[:, TOP_K:TOP_K + 1]
         + y1_ref[...].astype(F32) * rt[:, TOP_K + 1:TOP_K + 2])
    y = x * lax.rsqrt(jnp.mean(x * x, axis=-1, keepdims=True) + RMS_EPS)
    o_ref[...] = y * g_ref[...]


def _combine_norm(h, y0, y1, route, gain, *, row_off, n_rows, tr=256):
    d = h.shape[1]
    off = row_off // tr
    row_spec = pl.BlockSpec((tr, d), lambda i: (i + off, 0))
    return pl.pallas_call(
        _combine_norm_body, grid=(n_rows // tr,),
        in_specs=[row_spec, row_spec, row_spec,
                  pl.BlockSpec((tr, route.shape[1]), lambda i: (i + off, 0)),
                  pl.BlockSpec((1, d), lambda i: (0, 0))],
        out_specs=pl.BlockSpec((tr, d), lambda i: (i, 0)),
        out_shape=jax.ShapeDtypeStruct((n_rows, d), F32),
        compiler_params=_cparams(1), name="combine_norm")(h, y0, y1, route, gain)


def _embed_norm_body(xp_ref, xs_ref, g_ref, x_ref, o_ref, *, n_prompt_tiles):
    def emit(x):
        x_ref[...] = x
        y = x * lax.rsqrt(jnp.mean(x * x, axis=-1, keepdims=True) + RMS_EPS)
        o_ref[...] = (y * g_ref[...]).astype(o_ref.dtype)

    @pl.when(pl.program_id(0) < n_prompt_tiles)
    def _():
        emit(xp_ref[...])

    @pl.when(pl.program_id(0) >= n_prompt_tiles)
    def _():
        emit(xs_ref[...])


def _embed_norm(x_prompt, x_sample, gain, *, tr=256):
    d = x_prompt.shape[1]
    n_p, n_s = x_prompt.shape[0] // tr, x_sample.shape[0] // tr
    out_spec = pl.BlockSpec((tr, d), lambda i: (i, 0))
    return pl.pallas_call(
        functools.partial(_embed_norm_body, n_prompt_tiles=n_p), grid=(n_p + n_s,),
        in_specs=[pl.BlockSpec((tr, d), lambda i: (jnp.minimum(i, n_p - 1), 0)),
                  pl.BlockSpec((tr, d), lambda i: (jnp.maximum(i - n_p, 0), 0)),
                  pl.BlockSpec((1, d), lambda i: (0, 0))],
        out_specs=[out_spec, out_spec],
        out_shape=[jax.ShapeDtypeStruct(((n_p + n_s) * tr, d), F32),
                   jax.ShapeDtypeStruct(((n_p + n_s) * tr, d), BF16)],
        compiler_params=_cparams(1), name="embed_norm")(x_prompt, x_sample, gain)


def _rope_tile(acc, cos_ref, sin_ref):
    cos = cos_ref[...]
    sin = sin_ref[...]
    lane = lax.broadcasted_iota(jnp.int32, (acc.shape[0], LANE), 1)
    first_half = (lane % SWA_HEAD_DIM) < (SWA_HEAD_DIM // 2)
    outs = []
    for j in range(acc.shape[1] // LANE):
        blk = acc[:, j * LANE:(j + 1) * LANE]
        rot = jnp.where(first_half,
                        pltpu.roll(blk, LANE - SWA_HEAD_DIM // 2, 1),
                        pltpu.roll(blk, SWA_HEAD_DIM // 2, 1))
        outs.append(blk * cos + rot * sin)
    return jnp.concatenate(outs, axis=1)


def _mm_body(*refs, has_bias, has_res, has_rope, w_rows_are_cols):
    it = iter(refs)
    x_ref = next(it)
    w_ref = next(it)
    b_ref = next(it) if has_bias else None
    cos_ref = next(it) if has_rope else None
    sin_ref = next(it) if has_rope else None
    res_ref = next(it) if has_res else None
    o_ref = next(it)
    wbf_ref = next(it)

    @pl.when(pl.program_id(1) == 0)
    def _():
        w = w_ref[...]
        wbf_ref[...] = (jnp.transpose(w) if w_rows_are_cols else w).astype(BF16)

    acc = jnp.dot(x_ref[...], wbf_ref[...], preferred_element_type=F32)
    if has_bias:
        acc = acc + b_ref[...]
    if has_rope:
        acc = _rope_tile(acc, cos_ref, sin_ref)
    if has_res:
        acc = acc + res_ref[...]
    o_ref[...] = acc.astype(o_ref.dtype)


def _mm(x, w, *, n_cols, tn, tm=TM_TOK, col_off=0, bias=None, res=None, rope=None,
        out_dtype=F32, w_rows_are_cols=False):
    m, k = x.shape
    cb = col_off // tn
    if w_rows_are_cols:
        w_spec = pl.BlockSpec((tn, k), lambda n, i: (n + cb, 0))
    else:
        w_spec = pl.BlockSpec((k, tn), lambda n, i: (0, n + cb))
    in_specs = [pl.BlockSpec((tm, k), lambda n, i: (i, 0)), w_spec]
    args = [x, w]
    if bias is not None:
        in_specs.append(pl.BlockSpec((1, tn), lambda n, i: (0, n + cb)))
        args.append(bias)
    if rope is not None:
        for t in rope:
            in_specs.append(pl.BlockSpec((tm, LANE), lambda n, i: (i, 0)))
            args.append(t)
    if res is not None:
        in_specs.append(pl.BlockSpec((tm, tn), lambda n, i: (i, n)))
        args.append(res)
    body = functools.partial(_mm_body, has_bias=bias is not None, has_res=res is not None,
                             has_rope=rope is not None, w_rows_are_cols=w_rows_are_cols)
    return pl.pallas_call(
        body, grid=(n_cols // tn, m // tm), in_specs=in_specs,
        out_specs=pl.BlockSpec((tm, tn), lambda n, i: (i, n)),
        out_shape=jax.ShapeDtypeStruct((m, n_cols), out_dtype),
        scratch_shapes=[pltpu.VMEM((k, tn), BF16)],
        compiler_params=_cparams(2), name="matmul_ws")(*args)


def _swiglu_tile(x, wg, wu):
    g = jnp.dot(x, wg, preferred_element_type=F32)
    u = jnp.dot(x, wu, preferred_element_type=F32)
    return g * jax.nn.sigmoid(g) * u


def _gateup_body(x_ref, wg_ref, wu_ref, o_ref, wgbf_ref, wubf_ref):
    @pl.when(pl.program_id(1) == 0)
    def _():
        wgbf_ref[...] = wg_ref[...].astype(BF16)
        wubf_ref[...] = wu_ref[...].astype(BF16)

    o_ref[...] = _swiglu_tile(x_ref[...], wgbf_ref[...], wubf_ref[...]).astype(o_ref.dtype)


def _gateup(x, wg, wu, *, tm=TM_TOK, tn=256):
    m, k = x.shape
    f = wg.shape[1]
    return pl.pallas_call(
        _gateup_body, grid=(f // tn, m // tm),
        in_specs=[pl.BlockSpec((tm, k), lambda n, i: (i, 0)),
                  pl.BlockSpec((k, tn), lambda n, i: (0, n)),
                  pl.BlockSpec((k, tn), lambda n, i: (0, n))],
        out_specs=pl.BlockSpec((tm, tn), lambda n, i: (i, n)),
        out_shape=jax.ShapeDtypeStruct((m, f), BF16),
        scratch_shapes=[pltpu.VMEM((k, tn), BF16), pltpu.VMEM((k, tn), BF16)],
        compiler_params=_cparams(2), name="ffn_gateup")(x, wg, wu)


def _for_row_pieces(nrows, fn, fn_first):
    nfull = nrows // MOE_SUB
    rem = nrows - nfull * MOE_SUB

    @pl.when(nfull >= MOE_FUSED)
    def _():
        fn_first(pl.ds(0, MOE_FUSED * MOE_SUB))

    for j in range(MOE_TILE // MOE_SUB):
        live = j < nfull
        if j < MOE_FUSED:
            live = jnp.logical_and(nfull < MOE_FUSED, live)

        @pl.when(live)
        def _():
            (fn_first if j == 0 else fn)(pl.ds(j * MOE_SUB, MOE_SUB))
    start = pl.multiple_of(nfull * MOE_SUB, MOE_PIECE)
    for m in range(MOE_PIECE, MOE_SUB + 1, MOE_PIECE):
        @pl.when(jnp.logical_and(rem > m - MOE_PIECE, rem <= m))
        def _():
            fn(pl.ds(start, m))


def _pack_halves(y):
    n = y.shape[1] // 2
    lo = lax.bitcast_convert_type(y[:, :n].astype(BF16).astype(F32), jnp.uint32) >> 16
    hi = lax.bitcast_convert_type(y[:, n:].astype(BF16).astype(F32), jnp.uint32)
    return (hi & jnp.uint32(0xFFFF0000)) | lo


def _unpack_halves(u):
    lo = lax.bitcast_convert_type(u << 16, F32).astype(BF16)
    hi = lax.bitcast_convert_type(u & jnp.uint32(0xFFFF0000), F32).astype(BF16)
    return jnp.concatenate([lo, hi], axis=1)


def _moe_gateup_body(te_ref, src_ref, nsub_ref, nrows_ref, tok_ref, xp_hbm, wg_ref, wu_ref,
                     o_ref, x_ref, stage_ref, wgbf_ref, wubf_ref, sem):
    r = pl.program_id(0)
    n = pl.program_id(1)
    nsub = nsub_ref[r]

    def row_copy(tok, i):
        return pltpu.make_async_copy(xp_hbm.at[pl.ds(tok, 1)], stage_ref.at[pl.ds(i, 1)], sem)

    @pl.when(jnp.logical_and(r == 0, n == 0))
    def _():
        stage_ref[...] = jnp.zeros_like(stage_ref)

    @pl.when(jnp.logical_and(n == 0, nsub > 0))
    def _():
        for j in range(MOE_TILE // MOE_SUB):
            @pl.when(j < nsub)
            def _():
                live = jnp.clip(nrows_ref[r] - j * MOE_SUB, 0, MOE_SUB)

                def start(i, carry):
                    row_copy(tok_ref[0, j * MOE_SUB + i], i).start()
                    return carry

                def wait(i, carry):
                    row_copy(0, i).wait()
                    return carry

                @pl.when(live == MOE_SUB)
                def _():
                    lax.fori_loop(0, MOE_SUB, start, 0, unroll=8)
                    lax.fori_loop(0, MOE_SUB, wait, 0, unroll=8)

                @pl.when(live < MOE_SUB)
                def _():
                    lax.fori_loop(0, live, start, 0)
                    lax.fori_loop(0, live, wait, 0)

                x_ref[pl.ds(j * MOE_SUB, MOE_SUB), :] = _unpack_halves(stage_ref[...])

    @pl.when(nsub > 0)
    def _():
        nrows = nrows_ref[r]

        @pl.when(nrows < MOE_SUB)
        def _():
            wgbf_ref[...] = wg_ref[...].astype(BF16)
            wubf_ref[...] = wu_ref[...].astype(BF16)

        def first_piece(rows):
            wg = wg_ref[...].astype(BF16)
            wu = wu_ref[...].astype(BF16)
            wgbf_ref[...] = wg
            wubf_ref[...] = wu
            o_ref[rows, :] = _swiglu_tile(x_ref[rows, :], wg, wu).astype(o_ref.dtype)

        def piece(rows):
            o_ref[rows, :] = _swiglu_tile(x_ref[rows, :], wgbf_ref[...],
                                          wubf_ref[...]).astype(o_ref.dtype)

        _for_row_pieces(nrows, piece, first_piece)


def _moe_gateup(meta, slot_tok, xp, wg, wu, *, tn=256):
    n_tiles = slot_tok.shape[0]
    kp = xp.shape[1]
    k, f = wg.shape[1], wg.shape[2]
    nn = f // tn

    def col(n, r, ns):
        return jnp.where(ns[r] > 0, n, nn - 1)

    grid_spec = pltpu.PrefetchScalarGridSpec(
        num_scalar_prefetch=4, grid=(n_tiles, nn),
        in_specs=[pl.BlockSpec((None, 1, MOE_TILE), lambda r, n, te, src, ns, nr: (src[r], 0, 0),
                               memory_space=pltpu.SMEM),
                  pl.BlockSpec(memory_space=pl.ANY),
                  pl.BlockSpec((None, k, tn),
                               lambda r, n, te, src, ns, nr: (te[r], 0, col(n, r, ns))),
                  pl.BlockSpec((None, k, tn),
                               lambda r, n, te, src, ns, nr: (te[r], 0, col(n, r, ns)))],
        out_specs=pl.BlockSpec((MOE_TILE, tn),
                               lambda r, n, te, src, ns, nr: (src[r], col(n, r, ns))),
        scratch_shapes=[pltpu.VMEM((MOE_TILE, k), BF16), pltpu.VMEM((MOE_SUB, kp), jnp.uint32),
                        pltpu.VMEM((k, tn), BF16), pltpu.VMEM((k, tn), BF16),
                        pltpu.SemaphoreType.DMA(())])
    return pl.pallas_call(
        _moe_gateup_body, grid_spec=grid_spec,
        out_shape=jax.ShapeDtypeStruct((n_tiles * MOE_TILE, f), BF16),
        compiler_params=_cparams(2), name="moe_gateup")(*meta, slot_tok, xp, wg, wu)


def _down_body(a_ref, w_ref, res_ref, o_ref, acc_ref):
    kk = pl.program_id(2)

    @pl.when(kk == 0)
    def _():
        acc_ref[...] = jnp.zeros_like(acc_ref)

    acc_ref[...] += jnp.dot(a_ref[...], w_ref[...].astype(BF16), preferred_element_type=F32)

    @pl.when(kk == pl.num_programs(2) - 1)
    def _():
        o_ref[...] = (acc_ref[...] + res_ref[...]).astype(o_ref.dtype)


def _down(a, w, res, *, tm=TM_TOK, tn=1024, tk=2048):
    m, k = a.shape
    n = w.shape[1]
    return pl.pallas_call(
        _down_body, grid=(m // tm, n // tn, k // tk),
        in_specs=[pl.BlockSpec((tm, tk), lambda i, j, kk: (i, kk)),
                  pl.BlockSpec((tk, tn), lambda i, j, kk: (kk, j)),
                  pl.BlockSpec((tm, tn), lambda i, j, kk: (i, j))],
        out_specs=pl.BlockSpec((tm, tn), lambda i, j, kk: (i, j)),
        out_shape=jax.ShapeDtypeStruct((m, n), F32),
        scratch_shapes=[pltpu.VMEM((tm, tn), F32)],
        compiler_params=_cparams(3), name="ffn_down")(a, w, res)


def _moe_down_body(te_ref, src_ref, nsub_ref, nrows_ref, a_ref, w_ref, o_ref, acc_ref, wbf_ref):
    r = pl.program_id(0)
    kk = pl.program_id(2)
    nsub = nsub_ref[r]

    @pl.when(nsub > 0)
    def _():
        @pl.when(kk == 0)
        def _():
            acc_ref[...] = jnp.zeros_like(acc_ref)

        nrows = nrows_ref[r]

        @pl.when(nrows < MOE_SUB)
        def _():
            wbf_ref[...] = w_ref[...].astype(BF16)

        def first_piece(rows):
            w = w_ref[...].astype(BF16)
            wbf_ref[...] = w
            acc_ref[rows, :] += jnp.dot(a_ref[rows, :], w, preferred_element_type=F32)

        def piece(rows):
            acc_ref[rows, :] += jnp.dot(a_ref[rows, :], wbf_ref[...],
                                        preferred_element_type=F32)

        _for_row_pieces(nrows, piece, first_piece)

        @pl.when(kk == pl.num_programs(2) - 1)
        def _():
            o_ref[...] = acc_ref[...].astype(o_ref.dtype)


def _moe_down(meta, a, w, *, tn=1024, tk=1024):
    n_slots, k = a.shape
    n = w.shape[2]
    nn, nk = n // tn, k // tk

    def a_map(r, j, kk, te, src, ns, nr):
        return (src[r], jnp.where(ns[r] > 0, kk, nk - 1))

    def w_map(r, j, kk, te, src, ns, nr):
        live = ns[r] > 0
        return (te[r], jnp.where(live, kk, nk - 1), jnp.where(live, j, nn - 1))

    def o_map(r, j, kk, te, src, ns, nr):
        return (src[r], jnp.where(ns[r] > 0, j, nn - 1))

    grid_spec = pltpu.PrefetchScalarGridSpec(
        num_scalar_prefetch=4, grid=(n_slots // MOE_TILE, nn, nk),
        in_specs=[pl.BlockSpec((MOE_TILE, tk), a_map), pl.BlockSpec((None, tk, tn), w_map)],
        out_specs=pl.BlockSpec((MOE_TILE, tn), o_map),
        scratch_shapes=[pltpu.VMEM((MOE_TILE, tn), F32), pltpu.VMEM((tk, tn), BF16)])
    return pl.pallas_call(
        _moe_down_body, grid_spec=grid_spec,
        out_shape=jax.ShapeDtypeStruct((n_slots, n), BF16),
        compiler_params=_cparams(3), name="moe_down")(*meta, a, w)


def _gla_body(q_ref, k_ref, v_ref, g_ref, a_ref, wa_ref, ba_ref, gn_ref, s0_ref,
              o_ref, s_prompt_ref, s_sample_ref, s_ref):
    c = pl.program_id(0)

    @pl.when(c == 0)
    def _():
        s_ref[...] = jnp.zeros_like(s_ref)

    @pl.when(c >= N_PROMPT_CHUNKS)
    def _():
        s_ref[...] = s0_ref[...]

    xg = jnp.dot(a_ref[...].astype(BF16), wa_ref[...].astype(BF16),
                 preferred_element_type=F32) + ba_ref[...]
    la = (jnp.minimum(xg, 0.0) - jnp.log1p(jnp.exp(-jnp.abs(xg)))) * (1.0 / GLA_GATE_NORM)

    row = lax.broadcasted_iota(jnp.int32, (CHUNK, CHUNK), 0)
    col = lax.broadcasted_iota(jnp.int32, (CHUNK, CHUNK), 1)
    causal = row >= col
    tri = causal.astype(BF16)
    la_hi = la.astype(BF16)
    la_lo = (la - la_hi.astype(F32)).astype(BF16)
    cum = (jnp.dot(tri, la_hi, preferred_element_type=F32)
           + jnp.dot(tri, la_lo, preferred_element_type=F32))
    last = cum[CHUNK - 1:CHUNK, :]

    q = q_ref[...].astype(F32) * (GLA_DK ** -0.5)
    q_in = q * jnp.exp(cum)
    k_out = (k_ref[...].astype(F32) * jnp.exp(last - cum)).astype(BF16)
    q_loc = (q_in * jnp.exp(-last)).astype(BF16)
    q_in = q_in.astype(BF16)
    dec_rows = jnp.broadcast_to(jnp.exp(last), (LANE, QK_A))

    for h in range(GLA_HEADS):
        dk = slice(h * GLA_DK, (h + 1) * GLA_DK)
        dv = slice(h * GLA_DV, (h + 1) * GLA_DV)
        v = v_ref[:, dv]
        g = g_ref[:, dv].astype(F32)
        s_old = s_ref[h]
        attn = lax.dot_general(q_loc[:, dk], k_out[:, dk], (((1,), (1,)), ((), ())),
                               preferred_element_type=F32)
        attn = jnp.where(causal, attn, 0.0)
        o = (jnp.dot(q_in[:, dk], s_old.astype(BF16), preferred_element_type=F32)
             + jnp.dot(attn.astype(BF16), v, preferred_element_type=F32))
        kv = lax.dot_general(k_out[:, dk], v, (((0,), (0,)), ((), ())),
                             preferred_element_type=F32)
        dec_col = jnp.transpose(dec_rows[:, dk])
        dec = jnp.concatenate([dec_col] * (GLA_DV // LANE), axis=1)
        s_ref[h] = dec * s_old + kv
        on = o * lax.rsqrt(jnp.mean(o * o, axis=-1, keepdims=True) + RMS_EPS) * gn_ref[...]
        o_ref[:, dv] = (on * (g * jax.nn.sigmoid(g))).astype(o_ref.dtype)

    @pl.when(c == N_PROMPT_CHUNKS - 1)
    def _():
        s_prompt_ref[...] = s_ref[...]

    @pl.when(c >= N_PROMPT_CHUNKS)
    def _():
        s_sample_ref[...] = s_ref[...]


def _gla(proj, a_low, w_a2p, b_a, g_norm, s0):
    state_block = (None, GLA_HEADS, GLA_DK, GLA_DV)
    state_shape = (GLA_HEADS, GLA_DK, GLA_DV)

    def s0_map(c):
        return (jnp.maximum(c - N_PROMPT_CHUNKS, 0), 0, 0, 0)

    return pl.pallas_call(
        _gla_body, grid=(N_CHUNKS,),
        in_specs=[pl.BlockSpec((CHUNK, QK_A), lambda c: (c, 0)),
                  pl.BlockSpec((CHUNK, QK_A), lambda c: (c, 1)),
                  pl.BlockSpec((CHUNK, V_A), lambda c: (c, 1)),
                  pl.BlockSpec((CHUNK, V_A), lambda c: (c, 2)),
                  pl.BlockSpec((CHUNK, LANE), lambda c: (c, 0)),
                  pl.BlockSpec((LANE, QK_A), lambda c: (0, 0)),
                  pl.BlockSpec((1, QK_A), lambda c: (0, 0)),
                  pl.BlockSpec((1, GLA_DV), lambda c: (0, 0)),
                  pl.BlockSpec(state_block, s0_map)],
        out_specs=[pl.BlockSpec((CHUNK, V_A), lambda c: (c, 0)),
                   pl.BlockSpec(state_block, lambda c: (0, 0, 0, 0)),
                   pl.BlockSpec(state_block, s0_map)],
        out_shape=[jax.ShapeDtypeStruct((T_TOK, V_A), BF16),
                   jax.ShapeDtypeStruct((1,) + state_shape, F32),
                   jax.ShapeDtypeStruct((DEC_BATCH,) + state_shape, F32)],
        scratch_shapes=[pltpu.VMEM(state_shape, F32)],
        compiler_params=_cparams(1), name="gla")(
            proj, proj, proj, proj, a_low, w_a2p, b_a, g_norm, s0)


def _swa_body(q_ref, k0_ref, k1_ref, k2_ref, v0_ref, v1_ref, v2_ref, sink_ref, o_ref):
    c = pl.program_id(0)
    kcat = jnp.concatenate([k0_ref[...], k1_ref[...], k2_ref[...]], axis=0)
    vcat = jnp.concatenate([v0_ref[...], v1_ref[...], v2_ref[...]], axis=0)
    n_keys = (WIN_CHUNKS + 1) * CHUNK
    n_pairs = SWA_GROUP // 2
    n_q = n_pairs * CHUNK
    key_chunk = lax.broadcasted_iota(jnp.int32, (n_keys, n_q), 0) // CHUNK
    valid = jnp.logical_or(c >= N_PROMPT_CHUNKS, c - WIN_CHUNKS + key_chunk >= 0)
    low_half = lax.broadcasted_iota(jnp.int32, (n_keys, LANE), 1) < SWA_HEAD_DIM
    scale = SWA_HEAD_DIM ** -0.5

    for slab in range(KV_B // LANE):
        ks = kcat[:, slab * LANE:(slab + 1) * LANE]
        vs = vcat[:, slab * LANE:(slab + 1) * LANE]
        ks_sw = pltpu.roll(ks, SWA_HEAD_DIM, 1)
        vs_sw = pltpu.roll(vs, SWA_HEAD_DIM, 1)
        for par in range(2):
            kh = 2 * slab + par
            k_lo, k_hi = (ks, ks_sw) if par == 0 else (ks_sw, ks)
            v_lo, v_hi = (vs, vs_sw) if par == 0 else (vs_sw, vs)
            k_half = (jnp.where(low_half, k_lo, 0.0).astype(BF16),
                      jnp.where(low_half, 0.0, k_hi).astype(BF16))
            v_half = (jnp.where(low_half, v_lo, 0.0).astype(BF16),
                      jnp.where(low_half, 0.0, v_hi).astype(BF16))
            first = kh * n_pairs
            qs = jnp.concatenate([q_ref[:, (first + p) * LANE:(first + p + 1) * LANE]
                                  for p in range(n_pairs)], axis=0)
            acc = None
            for parity in range(2):
                s = lax.dot_general(k_half[parity], qs, (((1,), (1,)), ((), ())),
                                    preferred_element_type=F32) * scale
                s = jnp.where(valid, s, -1e30)
                sink = sink_ref[kh, parity][0:1, :]
                m = jnp.maximum(jnp.max(s, axis=0, keepdims=True), sink)
                p = jnp.exp(s - m)
                denom = jnp.sum(p, axis=0, keepdims=True) + jnp.exp(sink - m)
                pn = (p * (1.0 / denom)).astype(BF16)
                o = lax.dot_general(pn, v_half[parity], (((0,), (0,)), ((), ())),
                                    preferred_element_type=F32)
                acc = o if acc is None else acc + o
            for p in range(n_pairs):
                o_ref[:, (first + p) * LANE:(first + p + 1) * LANE] = (
                    acc[p * CHUNK:(p + 1) * CHUNK, :].astype(o_ref.dtype))


def _swa(q, kf, vf, sink_rows):
    def kv_map(j):
        def index(c):
            prompt = jnp.maximum(c - WIN_CHUNKS + j, 0)
            sample = N_PROMPT_CHUNKS + (WIN_CHUNKS + 1) * (c - N_PROMPT_CHUNKS) + j
            return (jnp.where(c < N_PROMPT_CHUNKS, prompt, sample), 0)
        return index

    kv_specs = [pl.BlockSpec((CHUNK, KV_B), kv_map(j)) for j in range(WIN_CHUNKS + 1)]
    return pl.pallas_call(
        _swa_body, grid=(N_CHUNKS,),
        in_specs=[pl.BlockSpec((CHUNK, D_MODEL), lambda c: (c, 0))] + kv_specs + kv_specs
        + [pl.BlockSpec(sink_rows.shape, lambda c: (0, 0, 0, 0))],
        out_specs=pl.BlockSpec((CHUNK, D_MODEL), lambda c: (c, 0)),
        out_shape=jax.ShapeDtypeStruct((T_TOK, D_MODEL), BF16),
        compiler_params=_cparams(1), name="swa")(q, kf, kf, kf, vf, vf, vf, sink_rows)


def _route(route, counts):
    n_tok = route.shape[0]
    flat_e = route[:, 0:TOP_K].astype(jnp.int32).reshape(-1)
    rank = route[:, 2 * TOP_K:3 * TOP_K].astype(jnp.int32).reshape(-1)
    counts = counts[0, :N_EXPERTS].astype(jnp.int32)
    n_assign = n_tok * TOP_K
    n_tiles_e = (counts + MOE_TILE - 1) // MOE_TILE
    tile_end = jnp.cumsum(n_tiles_e)
    tile_start = tile_end - n_tiles_e
    dest = tile_start[flat_e] * MOE_TILE + rank
    n_tiles = -(-n_assign // MOE_TILE) + N_EXPERTS
    n_slots = n_tiles * MOE_TILE
    slot_tok = jnp.zeros((n_slots,), jnp.int32).at[dest].set(
        jnp.arange(n_assign, dtype=jnp.int32) // TOP_K)
    n_used = tile_end[-1]
    r = jnp.arange(n_tiles, dtype=jnp.int32)
    src = jnp.maximum(jnp.minimum(r, n_used - 1), 0)
    te = jnp.clip(jnp.searchsorted(tile_end, src, side='right'), 0, N_EXPERTS - 1).astype(jnp.int32)
    nrows = jnp.where(r < n_used,
                      jnp.clip(counts[te] - (r - tile_start[te]) * MOE_TILE, 0, MOE_TILE), 0)
    nsub = (nrows + MOE_SUB - 1) // MOE_SUB
    meta = (te, src.astype(jnp.int32), nsub.astype(jnp.int32), nrows.astype(jnp.int32))
    return meta, slot_tok.reshape(n_tiles, 1, MOE_TILE), dest.reshape(n_tok, TOP_K)


def kernel(x_prompt, x_sample, state_gla, cache_swa_k, cache_swa_v, ln_mix, ln_ff, ln_kv, ln_out, gla_w_in, gla_w_a2, gla_b_a, gla_norm, gla_w_o, kv_w, kv_b, swa_w_q, swa_b_q, swa_sinks, swa_w_o, swa_b_o, ffd_w_gate, ffd_w_up, ffd_w_down, moe_router, moe_w_gate, moe_w_up, moe_w_down):
    x, hn = _embed_norm(x_prompt.reshape(SEQ, D_MODEL),
                        x_sample.reshape(DEC_BATCH * DEC_SEQ, D_MODEL), ln_mix[0:1])
    w_in_t = jnp.swapaxes(gla_w_in[0], 0, 1)
    proj = _mm(hn, w_in_t, n_cols=GLA_MAIN, tn=512, out_dtype=BF16, w_rows_are_cols=True)
    w_low_t = jnp.pad(w_in_t[GLA_MAIN:, :], ((0, LANE - GLA_GATE_RANK), (0, 0)))
    a_low = _mm(hn, w_low_t, n_cols=LANE, tn=LANE, out_dtype=F32, w_rows_are_cols=True)
    w_a2p = jnp.pad(gla_w_a2[0], ((0, LANE - GLA_GATE_RANK), (0, 0)))
    o_gla, s_prompt, s_sample = _gla(proj, a_low, w_a2p, gla_b_a[0:1], gla_norm[0:1], state_gla[0])
    h = _mm(o_gla, gla_w_o[0], n_cols=D_MODEL, tn=512, res=x, out_dtype=F32)

    (hn,) = _rmsnorm(h, ln_ff[0:1])
    act = _gateup(hn, ffd_w_gate[0], ffd_w_up[0])
    h = _down(act, ffd_w_down[0], h)

    hkv, hn = _rmsnorm(h, jnp.stack([ln_kv, ln_mix[1]]))
    pos = jnp.concatenate([jnp.arange(SEQ, dtype=jnp.int32),
                           jnp.tile(PAST_LEN + jnp.arange(DEC_SEQ, dtype=jnp.int32), DEC_BATCH)])
    half = SWA_HEAD_DIM // 2
    inv_freq = ROPE_THETA ** (-jnp.arange(half, dtype=F32) / half)
    ang = pos.astype(F32)[:, None] * inv_freq[None, :]
    cos, sin = jnp.cos(ang), jnp.sin(ang)
    rope = (jnp.concatenate([cos, cos, cos, cos], axis=1),
            jnp.concatenate([-sin, sin, -sin, sin], axis=1))
    kv_bias = kv_b[None, :]
    k_new = _mm(hkv, kv_w, n_cols=KV_B, tn=512, bias=kv_bias, rope=rope, out_dtype=F32)
    v_new = _mm(hkv, kv_w, n_cols=KV_B, tn=512, col_off=KV_B, bias=kv_bias, out_dtype=F32)
    q = _mm(hn, swa_w_q[0], n_cols=D_MODEL, tn=512, bias=swa_b_q[0:1], rope=rope,
            out_dtype=BF16)

    cache_rows = cache_swa_k.shape[1]

    def with_cache(new, cache):
        return jnp.concatenate([cache.reshape(DEC_BATCH, cache_rows, KV_B),
                                new[SEQ:].reshape(DEC_BATCH, DEC_SEQ, KV_B)], axis=1)

    k_samp = with_cache(k_new, cache_swa_k)
    v_samp = with_cache(v_new, cache_swa_v)
    kf = jnp.concatenate([k_new[:SEQ], k_samp.reshape(-1, KV_B)], axis=0)
    vf = jnp.concatenate([v_new[:SEQ], v_samp.reshape(-1, KV_B)], axis=0)
    sinks = swa_sinks[0].reshape(SWA_KV_HEADS, SWA_GROUP // 2, 2).transpose(0, 2, 1)
    sink_rows = jnp.broadcast_to(
        jnp.repeat(sinks, CHUNK, axis=2)[:, :, None, :],
        (SWA_KV_HEADS, 2, 8, (SWA_GROUP // 2) * CHUNK))
    o_swa = _swa(q, kf, vf, sink_rows)
    h = _mm(o_swa, swa_w_o[0], n_cols=D_MODEL, tn=512, bias=swa_b_o[0:1], res=h, out_dtype=F32)

    router = jnp.pad(moe_router[0], ((0, 0), (0, LANE - N_EXPERTS)))
    hn_packed, route, counts = _rmsnorm(h, ln_ff[1:2], router=router, out_dtype=jnp.uint32,
                                        tr=256)
    meta, slot_tok, dest = _route(route, counts)
    act = _moe_gateup(meta, slot_tok, hn_packed, moe_w_gate[0], moe_w_up[0])
    yb = _moe_down(meta, act, moe_w_down[0])
    y0, y1 = yb[dest[:, 0]], yb[dest[:, 1]]
    y_prompt = _combine_norm(h, y0, y1, route, ln_out[None, :], row_off=0, n_rows=SEQ)
    y_sample = _combine_norm(h, y0, y1, route, ln_out[None, :], row_off=SEQ,
                             n_rows=DEC_BATCH * DEC_SEQ)

    kv_shape = (SWA_KV_HEADS, SWA_HEAD_DIM)
    return (y_prompt.reshape(1, SEQ, D_MODEL),
            y_sample.reshape(DEC_BATCH, DEC_SEQ, D_MODEL),
            s_prompt[None],
            k_new[SEQ - WINDOW:SEQ].reshape(1, WINDOW, *kv_shape),
            v_new[SEQ - WINDOW:SEQ].reshape(1, WINDOW, *kv_shape),
            s_sample[None],
            k_samp[:, -cache_rows:].reshape(DEC_BATCH, cache_rows, *kv_shape),
            v_samp[:, -cache_rows:].reshape(DEC_BATCH, cache_rows, *kv_shape))
```
